```python
import math
import jax, jax.numpy as jnp
from jax import lax
import numpy as np

D_MODEL = 1024
BATCH = 16
SEQ = 2048
DEPTH = 2

RET_HEADS = 4
RET_QK_W = D_MODEL // 2
RET_V_W = D_MODEL
RET_QK_DIM = RET_QK_W // RET_HEADS
RET_V_DIM = RET_V_W // RET_HEADS
RET_CHUNK = 128
ROPE_BASE = 10000.0
SB_HEADS = 8
SB_W = D_MODEL // 2
SB_HEAD_DIM = SB_W // SB_HEADS
SB_BLOCK = 128
SSM_W = D_MODEL // 2
SSM_GROUP = 16
SSM_GROUPS = SSM_W // SSM_GROUP
SSM_STATE = 64
DT_MIN = 1e-3
DT_MAX = 1e-1

EPS = 1e-6
IN_SIZES = (RET_QK_W, RET_QK_W, RET_V_W, RET_V_W,
            SB_W, SB_W, SB_W, SB_W,
            SSM_W, SSM_W,
            D_MODEL, D_MODEL, D_MODEL)
IN_W = sum(IN_SIZES)

kernel_name = "hybrid_retention_stickbreak_s5_gated"

F32 = jnp.float32


def rmsnorm(t, g):
    tf = t.astype(F32)
    return tf * lax.rsqrt(jnp.mean(tf * tf, axis=-1, keepdims=True) + EPS) * g.astype(F32)


def rope(t, cos, sin):
    half = t.shape[-1] // 2
    t1, t2 = t[..., :half], t[..., half:]
    return jnp.concatenate([t1 * cos - t2 * sin, t2 * cos + t1 * sin], axis=-1)


def retention(q, k, v, q_norm, k_norm, out_norm, cos, sin):
    B_, S_ = q.shape[:2]
    q = rope(rmsnorm(q, q_norm), cos, sin)
    k = rope(rmsnorm(k, k_norm), cos, sin) * (RET_QK_DIM ** -0.5)
    v = v.astype(F32)
    n = S_ // RET_CHUNK
    qc = q.reshape(B_, n, RET_CHUNK, RET_HEADS, RET_QK_DIM)
    kc = k.reshape(B_, n, RET_CHUNK, RET_HEADS, RET_QK_DIM)
    vc = v.reshape(B_, n, RET_CHUNK, RET_HEADS, RET_V_DIM)
    log_gamma = jnp.log1p(-jnp.exp2(-5.0 - jnp.arange(RET_HEADS, dtype=F32)))
    idx = jnp.arange(RET_CHUNK, dtype=F32)
    rel = idx[:, None] - idx[None, :]
    decay = jnp.where(rel >= 0, jnp.exp(jnp.maximum(rel, 0.0)[None] * log_gamma[:, None, None]), 0.0)
    scores = jnp.einsum('bnihd,bnjhd->bnhij', qc, kc) * decay[None, None]
    inner = jnp.einsum('bnhij,bnjhe->bnihe', scores, vc)
    k_decay = jnp.exp((RET_CHUNK - 1 - idx)[:, None] * log_gamma[None])
    q_decay = jnp.exp((idx + 1)[:, None] * log_gamma[None])
    chunk_kv = jnp.einsum('bnjhd,bnjhe->nbhde', kc * k_decay[:, :, None], vc)
    chunk_decay = jnp.exp(RET_CHUNK * log_gamma)[None, :, None, None]

    def step(state, kv):
        return state * chunk_decay + kv, state

    _, prev = lax.scan(step, jnp.zeros_like(chunk_kv[0]), chunk_kv)
    cross = jnp.einsum('bnihd,nbhde->bnihe', qc * q_decay[:, :, None], prev)
    out = (inner + cross).reshape(B_, S_, RET_HEADS, RET_V_DIM)
    out = rmsnorm(out, out_norm.reshape(RET_HEADS, RET_V_DIM))
    return out.reshape(B_, S_, RET_V_W)


def stick_breaking(q, k, v, q_norm, k_norm):
    B_, S_ = q.shape[:2]
    q = rmsnorm(q, q_norm)
    k = rmsnorm(k, k_norm)
    v = v.astype(F32)
    scale = SB_HEAD_DIM ** -0.5
    outs = []
    for blk in range(S_ // SB_BLOCK):
        start = blk * SB_BLOCK
        end = start + SB_BLOCK
        z = jnp.einsum('bqhd,bkhd->bhqk', q[:, start:end], k[:, :end]) * scale
        t_idx = start + jnp.arange(SB_BLOCK)
        s_idx = jnp.arange(end)
        causal = s_idx[None, :] < t_idx[:, None]
        log_keep = jnp.where(causal, jax.nn.log_sigmoid(-z), 0.0)
        after = lax.cumsum(log_keep, axis=3, reverse=True) - log_keep
        w = jnp.where(causal, jnp.exp(jax.nn.log_sigmoid(z) + after), 0.0)
        outs.append(jnp.einsum('bhqk,bkhd->bqhd', w, v[:, :end]))
    return jnp.concatenate(outs, axis=1).reshape(B_, S_, SB_W)


def _ssm_combine(left, right):
    a1r, a1i, b1r, b1i = left
    a2r, a2i, b2r, b2i = right
    return (a2r * a1r - a2i * a1i,
            a2r * a1i + a2i * a1r,
            a2r * b1r - a2i * b1i + b2r,
            a2r * b1i + a2i * b1r + b2i)


def s5_ssm(u, a_re, a_im, log_dt, b_re, b_im, c_re, c_im, d_skip, w_glu, b_glu):
    B_, S_ = u.shape[:2]
    u = u.astype(F32)
    ug = jnp.swapaxes(u.reshape(B_, S_, SSM_GROUPS, SSM_GROUP), 0, 1)
    a_re = a_re.astype(F32)
    a_im = a_im.astype(F32)
    dt = jnp.exp(log_dt.astype(F32))[:, None]
    mag = jnp.exp(dt * a_re)
    ab_re = mag * jnp.cos(dt * a_im)
    ab_im = mag * jnp.sin(dt * a_im)
    den = a_re * a_re + a_im * a_im
    nr = ab_re - 1.0
    coef_re = (nr * a_re + ab_im * a_im) / den
    coef_im = (ab_im * a_re - nr * a_im) / den
    b_re = b_re.astype(F32)
    b_im = b_im.astype(F32)
    bb_re = coef_re[..., None] * b_re - coef_im[..., None] * b_im
    bb_im = coef_re[..., None] * b_im + coef_im[..., None] * b_re
    bu_re = jnp.einsum('sbgm,gpm->sbgp', ug, bb_re)
    bu_im = jnp.einsum('sbgm,gpm->sbgp', ug, bb_im)
    a_seq_re = jnp.broadcast_to(ab_re[None, None], (S_, 1, SSM_GROUPS, SSM_STATE))
    a_seq_im = jnp.broadcast_to(ab_im[None, None], (S_, 1, SSM_GROUPS, SSM_STATE))
    _, _, h_re, h_im = lax.associative_scan(_ssm_combine, (a_seq_re, a_seq_im, bu_re, bu_im), axis=0)
    y = (jnp.einsum('sbgp,gmp->sbgm', h_re, c_re.astype(F32))
         - jnp.einsum('sbgp,gmp->sbgm', h_im, c_im.astype(F32)))
    y = jnp.swapaxes(y, 0, 1).reshape(B_, S_, SSM_W) + d_skip.astype(F32) * u
    y = jax.nn.gelu(y)
    return y * jax.nn.sigmoid(y @ w_glu.astype(F32) + b_glu.astype(F32))


def hybrid_layer(x, norm_g, w_in, ret_q_norm, ret_k_norm, ret_out_norm, sb_q_norm, sb_k_norm,
                 ssm_a_re, ssm_a_im, ssm_log_dt, ssm_b_re, ssm_b_im, ssm_c_re, ssm_c_im,
                 ssm_d, ssm_w_glu, ssm_b_glu, proj_a, proj_b, proj_c, w_out, cos, sin):
    B_, S_, _ = x.shape
    h = rmsnorm(x, norm_g).astype(x.dtype)
    proj = h @ w_in
    points = []
    acc = 0
    for sz in IN_SIZES[:-1]:
        acc += sz
        points.append(acc)
    (rq, rk, rv, rz, sq, sk, sv, sz_, cu, cz, ga, gb, gc) = jnp.split(proj, points, axis=-1)
    y_a = retention(rq.reshape(B_, S_, RET_HEADS, RET_QK_DIM),
                    rk.reshape(B_, S_, RET_HEADS, RET_QK_DIM),
                    rv.reshape(B_, S_, RET_HEADS, RET_V_DIM),
                    ret_q_norm, ret_k_norm, ret_out_norm, cos, sin) * jax.nn.silu(rz.astype(F32))
    y_b = stick_breaking(sq.reshape(B_, S_, SB_HEADS, SB_HEAD_DIM),
                         sk.reshape(B_, S_, SB_HEADS, SB_HEAD_DIM),
                         sv.reshape(B_, S_, SB_HEADS, SB_HEAD_DIM),
                         sb_q_norm, sb_k_norm) * jax.nn.silu(sz_.astype(F32))
    y_c = s5_ssm(cu, ssm_a_re, ssm_a_im, ssm_log_dt, ssm_b_re, ssm_b_im, ssm_c_re, ssm_c_im,
                 ssm_d, ssm_w_glu, ssm_b_glu) * jax.nn.silu(cz.astype(F32))
    merged = (jax.nn.sigmoid(ga.astype(F32)) * (y_a @ proj_a.astype(F32))
              + jax.nn.sigmoid(gb.astype(F32)) * (y_b @ proj_b.astype(F32))
              + jax.nn.sigmoid(gc.astype(F32)) * (y_c @ proj_c.astype(F32)))
    return x + (merged @ w_out.astype(F32)).astype(x.dtype)


def setup_inputs(seed: int = 0) -> dict:
    key = jax.random.key(seed)
    ks = jax.random.split(key, 24)
    L = DEPTH

    def nrm(k, shape, std):
        return jax.random.normal(k, shape, F32) * std

    def gain(k, shape):
        return 1.0 + 0.05 * jax.random.normal(k, shape, F32)

    n_idx = jnp.arange(SSM_STATE, dtype=F32)
    a_re = -0.5 + 0.01 * jax.random.normal(ks[8], (L, SSM_GROUPS, SSM_STATE), F32)
    a_im = math.pi * n_idx + 0.01 * jax.random.normal(ks[9], (L, SSM_GROUPS, SSM_STATE), F32)
    log_dt = jax.random.uniform(ks[10], (L, SSM_GROUPS), F32, math.log(DT_MIN), math.log(DT_MAX))
    return {
        "x": jax.random.normal(ks[0], (BATCH, SEQ, D_MODEL), F32),
        "norm_g": gain(ks[1], (L, D_MODEL)),
        "w_in": nrm(ks[2], (L, D_MODEL, IN_W), D_MODEL ** -0.5),
        "ret_q_norm": gain(ks[3], (L, RET_QK_DIM)),
        "ret_k_norm": gain(ks[4], (L, RET_QK_DIM)),
        "ret_out_norm": gain(ks[5], (L, RET_V_W)),
        "sb_q_norm": gain(ks[6], (L, SB_HEAD_DIM)),
        "sb_k_norm": gain(ks[7], (L, SB_HEAD_DIM)),
        "ssm_a_re": a_re,
        "ssm_a_im": a_im,
        "ssm_log_dt": log_dt,
        "ssm_b_re": nrm(ks[11], (L, SSM_GROUPS, SSM_STATE, SSM_GROUP), (2 * SSM_GROUP) ** -0.5),
        "ssm_b_im": nrm(ks[12], (L, SSM_GROUPS, SSM_STATE, SSM_GROUP), (2 * SSM_GROUP) ** -0.5),
        "ssm_c_re": nrm(ks[13], (L, SSM_GROUPS, SSM_GROUP, SSM_STATE), SSM_STATE ** -0.5),
        "ssm_c_im": nrm(ks[14], (L, SSM_GROUPS, SSM_GROUP, SSM_STATE), SSM_STATE ** -0.5),
        "ssm_d": nrm(ks[15], (L, SSM_W), 1.0),
        "ssm_w_glu": nrm(ks[16], (L, SSM_W, SSM_W), SSM_W ** -0.5),
        "ssm_b_glu": nrm(ks[17], (L, SSM_W), 0.01),
        "proj_a": nrm(ks[18], (L, RET_V_W, D_MODEL), RET_V_W ** -0.5),
        "proj_b": nrm(ks[19], (L, SB_W, D_MODEL), SB_W ** -0.5),
        "proj_c": nrm(ks[20], (L, SSM_W, D_MODEL), SSM_W ** -0.5),
        "w_out": nrm(ks[21], (L, D_MODEL, D_MODEL), D_MODEL ** -0.5),
    }


def reference(x, norm_g, w_in, ret_q_norm, ret_k_norm, ret_out_norm, sb_q_norm, sb_k_norm,
              ssm_a_re, ssm_a_im, ssm_log_dt, ssm_b_re, ssm_b_im, ssm_c_re, ssm_c_im,
              ssm_d, ssm_w_glu, ssm_b_glu, proj_a, proj_b, proj_c, w_out):
    S_ = x.shape[1]
    half = RET_QK_DIM // 2
    inv_freq = ROPE_BASE ** (-jnp.arange(half, dtype=F32) / half)
    ang = jnp.arange(S_, dtype=F32)[:, None] * inv_freq[None, :]
    cos = jnp.cos(ang)[:, None, :]
    sin = jnp.sin(ang)[:, None, :]
    for l in range(DEPTH):
        x = hybrid_layer(x, norm_g[l], w_in[l], ret_q_norm[l], ret_k_norm[l], ret_out_norm[l],
                         sb_q_norm[l], sb_k_norm[l], ssm_a_re[l], ssm_a_im[l], ssm_log_dt[l],
                         ssm_b_re[l], ssm_b_im[l], ssm_c_re[l], ssm_c_im[l], ssm_d[l],
                         ssm_w_glu[l], ssm_b_glu[l], proj_a[l], proj_b[l], proj_c[l], w_out[l],
                         cos, sin)
    return x
```

```python
import functools
import math

import jax
import jax.numpy as jnp
from jax import lax
from jax.experimental import pallas as pl
from jax.experimental.pallas import tpu as pltpu

F32 = jnp.float32
BF16 = jnp.bfloat16

D_MODEL = 1024
RET_HEADS = 4
RET_QK_W = 512
RET_V_W = 1024
RET_QK_DIM = 128
RET_V_DIM = 256
RET_CHUNK = 128
ROPE_BASE = 10000.0
SB_HEADS = 8
SB_W = 512
SB_HEAD_DIM = 64
SB_BLOCK = 128
SSM_W = 512
SSM_GROUP = 16
SSM_GROUPS = 32
SSM_STATE = 64
SSM_STATES = SSM_GROUPS * SSM_STATE
EPS = 1e-6
IN_W = 9216

COL_TILE = 1024
SSM_COL_TILE = 5
P_W = IN_W - COL_TILE
P_RQ, P_RK = 0, 1
P_RV, P_RZ = 1, 2
P_SQ, P_SK, P_SV, P_SZ = 24, 28, 32, 36
P_GA, P_GB, P_GC = 5, 6, 7

SSM_T = 32
SSM_LANE_CHUNK = 512
VMEM_LIMIT = 56 * 1024 * 1024


def _sigmoid(v):
    return 1.0 / (1.0 + jnp.exp(-v))


def _nt_dot(a, b):
    return lax.dot_general(a, b, (((1,), (1,)), ((), ())), preferred_element_type=F32)


def _dot(a, b):
    return jnp.dot(a, b, preferred_element_type=F32)


def _inproj_kernel(x_ref, g_ref, w_ref, p_ref, u_ref, h_scr):
    j = pl.program_id(2)
    tm = x_ref.shape[0]
    rc = min(256, tm)

    @pl.when(j == 0)
    def _():
        def body(i, _):
            rows = pl.ds(pl.multiple_of(i * rc, rc), rc)
            x = x_ref[rows, :]
            ms = jnp.mean(x * x, axis=-1, keepdims=True)
            h_scr[rows, :] = (x * lax.rsqrt(ms + EPS) * g_ref[...]).astype(BF16)
            return 0
        lax.fori_loop(0, tm // rc, body, 0)

    mc = min(512, tm)

    @pl.when(j != SSM_COL_TILE)
    def _():
        for r in range(0, tm, mc):
            p_ref[r:r + mc, :] = _dot(h_scr[r:r + mc, :], w_ref[...]).astype(BF16)

    @pl.when(j == SSM_COL_TILE)
    def _():
        for r in range(0, tm, mc):
            u_ref[r:r + mc, :] = _dot(h_scr[r:r + mc, :], w_ref[...]).astype(BF16)


def _inproj(x2d, g, w, B, S):
    tm = min(1024, S)
    nt = S // tm
    n_col = IN_W // COL_TILE
    return pl.pallas_call(
        _inproj_kernel,
        out_shape=(jax.ShapeDtypeStruct((B * S, P_W), BF16),
                   jax.ShapeDtypeStruct((S, B * COL_TILE), BF16)),
        grid=(B, nt, n_col),
        in_specs=[
            pl.BlockSpec((tm, D_MODEL), lambda b, t, j: (b * nt + t, 0)),
            pl.BlockSpec((1, D_MODEL), lambda b, t, j: (0, 0)),
            pl.BlockSpec((D_MODEL, COL_TILE), lambda b, t, j: (0, j)),
        ],
        out_specs=(
            pl.BlockSpec((tm, COL_TILE),
                         lambda b, t, j: (b * nt + t, j - (j >= SSM_COL_TILE).astype(jnp.int32))),
            pl.BlockSpec((tm, COL_TILE), lambda b, t, j: (t, b)),
        ),
        scratch_shapes=[pltpu.VMEM((tm, D_MODEL), BF16)],
        compiler_params=pltpu.CompilerParams(
            dimension_semantics=("arbitrary", "arbitrary", "arbitrary"),
            vmem_limit_bytes=VMEM_LIMIT),
        name="inproj",
    )(x2d, g, w)


def _ret_kernel(chunk_decay, q_ref, k_ref, v_ref, z_ref, cos_ref, sin_ref, dec_ref, qd_ref, kd_ref,
                qg_ref, kg_ref, og_ref, o_ref, st_ref):
    S = q_ref.shape[0]
    st_ref[...] = jnp.zeros_like(st_ref)

    def chunk(c, _):
        rows = pl.ds(pl.multiple_of(c * RET_CHUNK, RET_CHUNK), RET_CHUNK)
        cosf = cos_ref[rows, :]
        sinf = sin_ref[rows, :]
        for h in range(RET_HEADS):
            qk = slice(h * RET_QK_DIM, (h + 1) * RET_QK_DIM)
            vv = slice(h * RET_V_DIM, (h + 1) * RET_V_DIM)
            qh = q_ref[rows, qk].astype(F32)
            kh = k_ref[rows, qk].astype(F32)
            qn = qh * lax.rsqrt(jnp.mean(qh * qh, axis=-1, keepdims=True) + EPS) * qg_ref[...]
            kn = kh * lax.rsqrt(jnp.mean(kh * kh, axis=-1, keepdims=True) + EPS) * kg_ref[...]
            qr = qn * cosf + pltpu.roll(qn, RET_QK_DIM // 2, 1) * sinf
            kr = kn * cosf + pltpu.roll(kn, RET_QK_DIM // 2, 1) * sinf
            vh = v_ref[rows, vv]
            scores = _nt_dot(qr.astype(BF16), kr.astype(BF16)) * dec_ref[h]
            inner = _dot(scores.astype(BF16), vh)
            st = st_ref[h]
            cross = _dot((qr * qd_ref[h]).astype(BF16), st.astype(BF16))
            kdt = (kr * kd_ref[h]).T.astype(BF16)
            st_ref[h] = st * chunk_decay[h] + _dot(kdt, vh)
            o = inner + cross
            on = o * lax.rsqrt(jnp.mean(o * o, axis=-1, keepdims=True) + EPS) * og_ref[:, vv]
            zz = z_ref[rows, vv].astype(F32)
            o_ref[rows, vv] = (on * (zz * _sigmoid(zz))).astype(BF16)
        return 0

    lax.fori_loop(0, S // RET_CHUNK, chunk, 0)


def _retention(P, cosf, sinf, dec, qd, kd, chunk_decay, qg, kg, og, B, S):
    const2 = lambda b: (0, 0)
    const3 = lambda b: (0, 0, 0)
    return pl.pallas_call(
        functools.partial(_ret_kernel, chunk_decay),
        out_shape=jax.ShapeDtypeStruct((B * S, RET_V_W), BF16),
        grid=(B,),
        in_specs=[
            pl.BlockSpec((S, RET_QK_W), lambda b: (b, P_RQ)),
            pl.BlockSpec((S, RET_QK_W), lambda b: (b, P_RK)),
            pl.BlockSpec((S, RET_V_W), lambda b: (b, P_RV)),
            pl.BlockSpec((S, RET_V_W), lambda b: (b, P_RZ)),
            pl.BlockSpec((S, RET_QK_DIM), const2),
            pl.BlockSpec((S, RET_QK_DIM), const2),
            pl.BlockSpec((RET_HEADS, RET_CHUNK, RET_CHUNK), const3),
            pl.BlockSpec((RET_HEADS, RET_CHUNK, RET_QK_DIM), const3),
            pl.BlockSpec((RET_HEADS, RET_CHUNK, RET_QK_DIM), const3),
            pl.BlockSpec((1, RET_QK_DIM), const2),
            pl.BlockSpec((1, RET_QK_DIM), const2),
            pl.BlockSpec((1, RET_V_W), const2),
        ],
        out_specs=pl.BlockSpec((S, RET_V_W), lambda b: (b, 0)),
        scratch_shapes=[pltpu.VMEM((RET_HEADS, RET_QK_DIM, RET_V_DIM), F32)],
        compiler_params=pltpu.CompilerParams(
            dimension_semantics=("arbitrary",), vmem_limit_bytes=VMEM_LIMIT),
        name="retention",
    )(P, P, P, P, cosf, sinf, dec, qd, kd, qg, kg, og)


def _sb_kernel(q_ref, k_ref, v_ref, z_ref, qg_ref, kg_ref, tri_ref, o_ref,
               qlo_scr, qhi_scr, kn_scr, vlo_scr, vhi_scr, carry_scr, acc_scr):
    S = q_ref.shape[0]
    nc = min(256, S)
    lane_n = lax.broadcasted_iota(jnp.int32, (nc, 2 * SB_HEAD_DIM), 1)
    lo_n = lane_n < SB_HEAD_DIM
    inv_d = 1.0 / SB_HEAD_DIM

    def head_rms(ref, rows, g_ref):
        x = ref[rows, :].astype(F32)
        xx = x * x
        s_all = jnp.sum(xx, axis=-1, keepdims=True)
        s_lo = jnp.sum(jnp.where(lo_n, xx, 0.0), axis=-1, keepdims=True)
        ms = jnp.where(lo_n, s_lo, s_all - s_lo) * inv_d
        return x * lax.rsqrt(ms + EPS) * g_ref[...]

    def norm_body(i, _):
        rows = pl.ds(pl.multiple_of(i * nc, nc), nc)
        qn = head_rms(q_ref, rows, qg_ref) * (SB_HEAD_DIM ** -0.5)
        qlo_scr[rows, :] = jnp.where(lo_n, qn, 0.0).astype(BF16)
        qhi_scr[rows, :] = jnp.where(lo_n, 0.0, qn).astype(BF16)
        kn_scr[rows, :] = head_rms(k_ref, rows, kg_ref).astype(BF16)
        v = v_ref[rows, :].astype(F32)
        vlo_scr[rows, :] = jnp.where(lo_n, v, 0.0).astype(BF16)
        vhi_scr[rows, :] = jnp.where(lo_n, 0.0, v).astype(BF16)
        return 0

    lax.fori_loop(0, S // nc, norm_body, 0)

    qi_idx = lax.broadcasted_iota(jnp.int32, (SB_BLOCK, SB_BLOCK), 0)
    ki_idx = lax.broadcasted_iota(jnp.int32, (SB_BLOCK, SB_BLOCK), 1)
    causal = ki_idx < qi_idx

    def kblock(qrows, j, masked):
        krows = pl.ds(pl.multiple_of(j * SB_BLOCK, SB_BLOCK), SB_BLOCK)
        k2 = kn_scr[krows, :]
        ws = []
        for hh, q_scr in enumerate((qlo_scr, qhi_scr)):
            z = _nt_dot(q_scr[qrows, :], k2)
            lk = jnp.minimum(-z, 0.0) - jnp.log(1.0 + jnp.exp(-jnp.abs(z)))
            if masked:
                lk = jnp.where(causal, lk, 0.0)
            hi = lk.astype(BF16)
            lo = (lk - hi.astype(F32)).astype(BF16)
            cs = _dot(jnp.concatenate([hi, lo], axis=1), tri_ref[...])
            c_old = carry_scr[hh]
            after = cs[:, :SB_BLOCK] + c_old
            carry_scr[hh] = cs[:, SB_BLOCK:] + c_old
            w = jnp.exp(z + lk + after)
            if masked:
                w = jnp.where(causal, w, 0.0)
            ws.append(w.astype(BF16))
        wcat = jnp.concatenate(ws, axis=1)
        vcat = jnp.concatenate([vlo_scr[krows, :], vhi_scr[krows, :]], axis=0)
        acc_scr[...] += _dot(wcat, vcat)

    def qblock(qi, _):
        qrows = pl.ds(pl.multiple_of(qi * SB_BLOCK, SB_BLOCK), SB_BLOCK)
        carry_scr[...] = jnp.zeros_like(carry_scr)
        acc_scr[...] = jnp.zeros_like(acc_scr)
        kblock(qrows, qi, True)

        def off_diag(jj, _):
            kblock(qrows, qi - 1 - jj, False)
            return 0

        lax.fori_loop(0, qi, off_diag, 0)
        zz = z_ref[qrows, :].astype(F32)
        o_ref[qrows, :] = (acc_scr[...] * (zz * _sigmoid(zz))).astype(BF16)
        return 0

    lax.fori_loop(0, S // SB_BLOCK, qblock, 0)


def _stick_breaking(P, qg, kg, tri, B, S):
    n_pairs = SB_HEADS // 2
    pw = 2 * SB_HEAD_DIM
    const2 = lambda b, p: (0, 0)
    return pl.pallas_call(
        _sb_kernel,
        out_shape=jax.ShapeDtypeStruct((B * S, SB_W), BF16),
        grid=(B, n_pairs),
        in_specs=[
            pl.BlockSpec((S, pw), lambda b, p: (b, P_SQ + p)),
            pl.BlockSpec((S, pw), lambda b, p: (b, P_SK + p)),
            pl.BlockSpec((S, pw), lambda b, p: (b, P_SV + p)),
            pl.BlockSpec((S, pw), lambda b, p: (b, P_SZ + p)),
            pl.BlockSpec((1, pw), const2),
            pl.BlockSpec((1, pw), const2),
            pl.BlockSpec((2 * SB_BLOCK, 2 * SB_BLOCK), const2),
        ],
        out_specs=pl.BlockSpec((S, pw), lambda b, p: (b, p)),
        scratch_shapes=[pltpu.VMEM((S, pw), BF16) for _ in range(5)]
        + [pltpu.VMEM((2, SB_BLOCK, SB_BLOCK), F32), pltpu.VMEM((SB_BLOCK, pw), F32)],
        compiler_params=pltpu.CompilerParams(
            dimension_semantics=("arbitrary", "arbitrary"), vmem_limit_bytes=VMEM_LIMIT),
        name="stickbreak",
    )(P, P, P, P, qg, kg, tri)


def _ssm_kernel(u_ref, are_ref, aim_ref, bre_ref, bim_ref, cre_ref, cimn_ref, d_ref, wg_ref, bg_ref,
                o_ref, bur_scr, bui_scr, hr_scr, hi_scr, cr_scr, ci_scr):
    T, B, _ = u_ref.shape
    rows_n = T * B
    half_k = SSM_W // 2
    half_n = SSM_STATES // 2

    @pl.when(pl.program_id(0) == 0)
    def _():
        cr_scr[...] = jnp.zeros_like(cr_scr)
        ci_scr[...] = jnp.zeros_like(ci_scr)

    u = u_ref[:, :, :SSM_W].reshape(rows_n, SSM_W)
    for c in range(2):
        uc = u[:, c * half_k:(c + 1) * half_k]
        bur_scr[:, c * half_n:(c + 1) * half_n] = _dot(uc, bre_ref[c])
        bui_scr[:, c * half_n:(c + 1) * half_n] = _dot(uc, bim_ref[c])

    for lc in range(SSM_STATES // SSM_LANE_CHUNK):
        ls = slice(lc * SSM_LANE_CHUNK, (lc + 1) * SSM_LANE_CHUNK)
        ar = jnp.broadcast_to(are_ref[:, ls], (B, SSM_LANE_CHUNK))
        ai = jnp.broadcast_to(aim_ref[:, ls], (B, SSM_LANE_CHUNK))

        def step(t, carry):
            hr, hi = carry
            rows = pl.ds(pl.multiple_of(t * B, B), B)
            nr = ar * hr - ai * hi + bur_scr[rows, ls]
            ni = ar * hi + ai * hr + bui_scr[rows, ls]
            hr_scr[rows, ls] = nr.astype(BF16)
            hi_scr[rows, ls] = ni.astype(BF16)
            return nr, ni

        hr, hi = lax.fori_loop(0, T, step, (cr_scr[:, ls], ci_scr[:, ls]), unroll=8)
        cr_scr[:, ls] = hr
        ci_scr[:, ls] = hi

    ys = []
    for c in range(2):
        ss = slice(c * half_n, (c + 1) * half_n)
        ys.append(_dot(hr_scr[:, ss], cre_ref[c]) + _dot(hi_scr[:, ss], cimn_ref[c]))
    y = jnp.concatenate(ys, axis=1) + d_ref[...] * u.astype(F32)
    y = 0.5 * y * (1.0 + jnp.tanh(math.sqrt(2.0 / math.pi) * (y + 0.044715 * (y * y * y))))
    y = y * _sigmoid(_dot(y.astype(BF16), wg_ref[...]) + bg_ref[...])
    cz = u_ref[:, :, SSM_W:].reshape(rows_n, SSM_W).astype(F32)
    o_ref[...] = (y * (cz * _sigmoid(cz))).astype(BF16).reshape(T, B, SSM_W)


def _ssm(U3, are, aim, bre, bim, cre, cimn, d, wg, bg, B, S):
    T = min(SSM_T, S)
    rows_n = T * B
    const2 = lambda t: (0, 0)
    const3 = lambda t: (0, 0, 0)
    return pl.pallas_call(
        _ssm_kernel,
        out_shape=jax.ShapeDtypeStruct((S, B, SSM_W), BF16),
        grid=(S // T,),
        in_specs=[
            pl.BlockSpec((T, B, COL_TILE), lambda t: (t, 0, 0)),
            pl.BlockSpec((1, SSM_STATES), const2),
            pl.BlockSpec((1, SSM_STATES), const2),
            pl.BlockSpec((2, SSM_W // 2, SSM_STATES // 2), const3),
            pl.BlockSpec((2, SSM_W // 2, SSM_STATES // 2), const3),
            pl.BlockSpec((2, SSM_STATES // 2, SSM_W // 2), const3),
            pl.BlockSpec((2, SSM_STATES // 2, SSM_W // 2), const3),
            pl.BlockSpec((1, SSM_W), const2),
            pl.BlockSpec((SSM_W, SSM_W), const2),
            pl.BlockSpec((1, SSM_W), const2),
        ],
        out_specs=pl.BlockSpec((T, B, SSM_W), lambda t: (t, 0, 0)),
        scratch_shapes=[
            pltpu.VMEM((rows_n, SSM_STATES), F32), pltpu.VMEM((rows_n, SSM_STATES), F32),
            pltpu.VMEM((rows_n, SSM_STATES), BF16), pltpu.VMEM((rows_n, SSM_STATES), BF16),
            pltpu.VMEM((B, SSM_STATES), F32), pltpu.VMEM((B, SSM_STATES), F32),
        ],
        compiler_params=pltpu.CompilerParams(
            dimension_semantics=("arbitrary",), vmem_limit_bytes=VMEM_LIMIT),
        name="ssm",
    )(U3, are, aim, bre, bim, cre, cimn, d, wg, bg)


def _merge_kernel(ya_ref, yb_ref, yc_ref, ga_ref, gb_ref, gc_ref, x_ref, pa_ref, pb_ref, pc_ref, wo_ref,
                  o_ref):
    tm = x_ref.shape[0]
    mc = min(512, tm)
    for r in range(0, tm, mc):
        rows = slice(r, r + mc)
        m = _sigmoid(ga_ref[rows, :].astype(F32)) * _dot(ya_ref[rows, :], pa_ref[...])
        m += _sigmoid(gb_ref[rows, :].astype(F32)) * _dot(yb_ref[rows, :], pb_ref[...])
        m += _sigmoid(gc_ref[rows, :].astype(F32)) * _dot(yc_ref[rows, :], pc_ref[...])
        o_ref[rows, :] = x_ref[rows, :] + _dot(m.astype(BF16), wo_ref[...])


def _merge(ya, yb, yc2d, P, x2d, pa, pb, pc, wo, B, S):
    tm = min(1024, S)
    nt = S // tm
    tok = lambda b, t: (b * nt + t, 0)
    const2 = lambda b, t: (0, 0)
    return pl.pallas_call(
        _merge_kernel,
        out_shape=jax.ShapeDtypeStruct((B * S, D_MODEL), F32),
        grid=(B, nt),
        in_specs=[
            pl.BlockSpec((tm, RET_V_W), tok),
            pl.BlockSpec((tm, SB_W), tok),
            pl.BlockSpec((tm, SSM_W), lambda b, t: (t, b)),
            pl.BlockSpec((tm, D_MODEL), lambda b, t: (b * nt + t, P_GA)),
            pl.BlockSpec((tm, D_MODEL), lambda b, t: (b * nt + t, P_GB)),
            pl.BlockSpec((tm, D_MODEL), lambda b, t: (b * nt + t, P_GC)),
            pl.BlockSpec((tm, D_MODEL), tok),
            pl.BlockSpec((RET_V_W, D_MODEL), const2),
            pl.BlockSpec((SB_W, D_MODEL), const2),
            pl.BlockSpec((SSM_W, D_MODEL), const2),
            pl.BlockSpec((D_MODEL, D_MODEL), const2),
        ],
        out_specs=pl.BlockSpec((tm, D_MODEL), tok),
        compiler_params=pltpu.CompilerParams(
            dimension_semantics=("arbitrary", "arbitrary"), vmem_limit_bytes=VMEM_LIMIT),
        name="merge",
    )(ya, yb, yc2d, P, P, P, x2d, pa, pb, pc, wo)


def _rope_tables(S):
    half = RET_QK_DIM // 2
    inv_freq = ROPE_BASE ** (-jnp.arange(half, dtype=F32) / half)
    ang = jnp.arange(S, dtype=F32)[:, None] * inv_freq[None, :]
    cos, sin = jnp.cos(ang), jnp.sin(ang)
    return jnp.concatenate([cos, cos], axis=1), jnp.concatenate([-sin, sin], axis=1)


def _decay_tables():
    log_gamma = jnp.log1p(-jnp.exp2(-5.0 - jnp.arange(RET_HEADS, dtype=F32)))
    idx = jnp.arange(RET_CHUNK, dtype=F32)
    rel = idx[:, None] - idx[None, :]
    dec = jnp.where(rel >= 0, jnp.exp(jnp.maximum(rel, 0.0)[None] * log_gamma[:, None, None]), 0.0)
    kdec = jnp.exp((RET_CHUNK - 1 - idx)[None, :] * log_gamma[:, None])
    qdec = jnp.exp((idx + 1)[None, :] * log_gamma[:, None])
    full = lambda a: jnp.broadcast_to(a[:, :, None], (RET_HEADS, RET_CHUNK, RET_QK_DIM))
    chunk_decay = tuple(float(math.exp(RET_CHUNK * math.log1p(-2.0 ** (-5.0 - h)))) for h in range(RET_HEADS))
    return dec, full(qdec), full(kdec), chunk_decay


def _ssm_tables(a_re, a_im, log_dt, b_re, b_im, c_re, c_im):
    dt = jnp.exp(log_dt)[:, None]
    mag = jnp.exp(dt * a_re)
    ab_re = mag * jnp.cos(dt * a_im)
    ab_im = mag * jnp.sin(dt * a_im)
    den = a_re * a_re + a_im * a_im
    nr = ab_re - 1.0
    coef_re = (nr * a_re + ab_im * a_im) / den
    coef_im = (ab_im * a_re - nr * a_im) / den
    bb_re = coef_re[..., None] * b_re - coef_im[..., None] * b_im
    bb_im = coef_re[..., None] * b_im + coef_im[..., None] * b_re
    gl = SSM_GROUPS // 2
    eye = jnp.eye(gl, dtype=F32)

    def b_blocks(bb):
        return jnp.einsum('cgpm,gh->cgmhp', bb.reshape(2, gl, SSM_STATE, SSM_GROUP), eye).reshape(
            2, gl * SSM_GROUP, gl * SSM_STATE).astype(BF16)

    def c_blocks(cc):
        return jnp.einsum('cgmp,gh->cgphm', cc.reshape(2, gl, SSM_GROUP, SSM_STATE), eye).reshape(
            2, gl * SSM_STATE, gl * SSM_GROUP).astype(BF16)

    return (ab_re.reshape(1, SSM_STATES), ab_im.reshape(1, SSM_STATES),
            b_blocks(bb_re), b_blocks(bb_im), c_blocks(c_re), c_blocks(-c_im))


def _suffix_sum_matrix():
    k = jnp.arange(SB_BLOCK)
    upper = (k[:, None] > k[None, :]).astype(BF16)
    half = jnp.concatenate([upper, jnp.ones((SB_BLOCK, SB_BLOCK), BF16)], axis=1)
    return jnp.concatenate([half, half], axis=0)


def kernel(x, norm_g, w_in, ret_q_norm, ret_k_norm, ret_out_norm, sb_q_norm, sb_k_norm, ssm_a_re, ssm_a_im,
           ssm_log_dt, ssm_b_re, ssm_b_im, ssm_c_re, ssm_c_im, ssm_d, ssm_w_glu, ssm_b_glu, proj_a, proj_b,
           proj_c, w_out):
    B, S, D = x.shape
    depth = w_in.shape[0]
    assert D == D_MODEL and B == 16 and S % 128 == 0
    cosf, sinf = _rope_tables(S)
    dec, qd, kd, chunk_decay = _decay_tables()
    tri = _suffix_sum_matrix()
    x2d = x.reshape(B * S, D)
    for l in range(depth):
        P, U = _inproj(x2d, norm_g[l].reshape(1, D), w_in[l].astype(BF16), B, S)
        ya = _retention(P, cosf, sinf, dec, qd, kd, chunk_decay,
                        ret_q_norm[l].reshape(1, -1),
                        (ret_k_norm[l] * (RET_QK_DIM ** -0.5)).reshape(1, -1),
                        ret_out_norm[l].reshape(1, -1), B, S)
        yb = _stick_breaking(P, jnp.tile(sb_q_norm[l], 2).reshape(1, -1),
                             jnp.tile(sb_k_norm[l], 2).reshape(1, -1), tri, B, S)
        are, aim, bre, bim, cre, cimn = _ssm_tables(ssm_a_re[l], ssm_a_im[l], ssm_log_dt[l], ssm_b_re[l],
                                                    ssm_b_im[l], ssm_c_re[l], ssm_c_im[l])
        yc = _ssm(U.reshape(S, B, COL_TILE), are, aim, bre, bim, cre, cimn,
                  ssm_d[l].reshape(1, -1), ssm_w_glu[l].astype(BF16), ssm_b_glu[l].reshape(1, -1), B, S)
        x2d = _merge(ya, yb, yc.reshape(S, B * SSM_W), P, x2d, proj_a[l].astype(BF16),
                     proj_b[l].astype(BF16), proj_c[l].astype(BF16), w_out[l].astype(BF16), B, S)
    return x2d.reshape(B, S, D)
```

```python
import functools
import math

import jax
import jax.numpy as jnp
from jax import lax
from jax.experimental import pallas as pl
from jax.experimental.pallas import tpu as pltpu

F32 = jnp.float32
BF16 = jnp.bfloat16

D_MODEL = 1024
RET_HEADS = 4
RET_QK_W = 512
RET_V_W = 1024
RET_QK_DIM = 128
RET_V_DIM = 256
RET_CHUNK = 128
ROPE_BASE = 10000.0
SB_HEADS = 8
SB_W = 512
SB_HEAD_DIM = 64
SB_TILE = 256
SB_LANES = 128
SB_MASKED_LOGIT = -1e30
LOG2E = math.log2(math.e)
SSM_W = 512
SSM_GROUP = 16
SSM_GROUPS = 32
SSM_STATE = 64
SSM_STATES = SSM_GROUPS * SSM_STATE
EPS = 1e-6
IN_W = 9216

COL_TILE = 1024
SSM_COL_TILE = 5
P_W = IN_W - COL_TILE
P_RQ, P_RK = 0, 1
P_RV, P_RZ = 1, 2
P_SQ, P_SK, P_SV, P_SZ = 24, 28, 32, 36
P_GA, P_GB, P_GC = 5, 6, 7

SSM_T = 32
SSM_LANE_CHUNK = 512
VMEM_LIMIT = 56 * 1024 * 1024


def _sigmoid(v):
    return 1.0 / (1.0 + jnp.exp(-v))


def _nt_dot(a, b):
    return lax.dot_general(a, b, (((1,), (1,)), ((), ())), preferred_element_type=F32)


def _dot(a, b):
    return jnp.dot(a, b, preferred_element_type=F32)


def _inproj_kernel(x_ref, g_ref, w_ref, p_ref, u_ref, h_scr):
    j = pl.program_id(2)
    tm = x_ref.shape[0]
    rc = min(256, tm)

    @pl.when(j == 0)
    def _():
        def body(i, _):
            rows = pl.ds(pl.multiple_of(i * rc, rc), rc)
            x = x_ref[rows, :]
            ms = jnp.mean(x * x, axis=-1, keepdims=True)
            h_scr[rows, :] = (x * lax.rsqrt(ms + EPS) * g_ref[...]).astype(BF16)
            return 0
        lax.fori_loop(0, tm // rc, body, 0)

    mc = min(512, tm)

    @pl.when(j != SSM_COL_TILE)
    def _():
        for r in range(0, tm, mc):
            p_ref[r:r + mc, :] = _dot(h_scr[r:r + mc, :], w_ref[...]).astype(BF16)

    @pl.when(j == SSM_COL_TILE)
    def _():
        for r in range(0, tm, mc):
            u_ref[r:r + mc, :] = _dot(h_scr[r:r + mc, :], w_ref[...]).astype(BF16)


def _inproj(x2d, g, w, B, S):
    tm = min(1024, S)
    nt = S // tm
    n_col = IN_W // COL_TILE
    return pl.pallas_call(
        _inproj_kernel,
        out_shape=(jax.ShapeDtypeStruct((B * S, P_W), BF16),
                   jax.ShapeDtypeStruct((S, B * COL_TILE), BF16)),
        grid=(B, nt, n_col),
        in_specs=[
            pl.BlockSpec((tm, D_MODEL), lambda b, t, j: (b * nt + t, 0)),
            pl.BlockSpec((1, D_MODEL), lambda b, t, j: (0, 0)),
            pl.BlockSpec((D_MODEL, COL_TILE), lambda b, t, j: (0, j)),
        ],
        out_specs=(
            pl.BlockSpec((tm, COL_TILE),
                         lambda b, t, j: (b * nt + t, j - (j >= SSM_COL_TILE).astype(jnp.int32))),
            pl.BlockSpec((tm, COL_TILE), lambda b, t, j: (t, b)),
        ),
        scratch_shapes=[pltpu.VMEM((tm, D_MODEL), BF16)],
        compiler_params=pltpu.CompilerParams(
            dimension_semantics=("arbitrary", "arbitrary", "arbitrary"),
            vmem_limit_bytes=VMEM_LIMIT),
        name="inproj",
    )(x2d, g, w)


def _ret_kernel(chunk_decay, q_ref, k_ref, v_ref, z_ref, cos_ref, sin_ref, dec_ref, qd_ref, kd_ref,
                qg_ref, kg_ref, og_ref, o_ref, st_ref):
    S = q_ref.shape[0]
    st_ref[...] = jnp.zeros_like(st_ref)

    def chunk(c, _):
        rows = pl.ds(pl.multiple_of(c * RET_CHUNK, RET_CHUNK), RET_CHUNK)
        cosf = cos_ref[rows, :]
        sinf = sin_ref[rows, :]
        for h in range(RET_HEADS):
            qk = slice(h * RET_QK_DIM, (h + 1) * RET_QK_DIM)
            vv = slice(h * RET_V_DIM, (h + 1) * RET_V_DIM)
            qh = q_ref[rows, qk].astype(F32)
            kh = k_ref[rows, qk].astype(F32)
            qn = qh * lax.rsqrt(jnp.mean(qh * qh, axis=-1, keepdims=True) + EPS) * qg_ref[...]
            kn = kh * lax.rsqrt(jnp.mean(kh * kh, axis=-1, keepdims=True) + EPS) * kg_ref[...]
            qr = qn * cosf + pltpu.roll(qn, RET_QK_DIM // 2, 1) * sinf
            kr = kn * cosf + pltpu.roll(kn, RET_QK_DIM // 2, 1) * sinf
            vh = v_ref[rows, vv]
            scores = _nt_dot(qr.astype(BF16), kr.astype(BF16)) * dec_ref[h]
            inner = _dot(scores.astype(BF16), vh)
            st = st_ref[h]
            cross = _dot((qr * qd_ref[h]).astype(BF16), st.astype(BF16))
            kdt = (kr * kd_ref[h]).T.astype(BF16)
            st_ref[h] = st * chunk_decay[h] + _dot(kdt, vh)
            o = inner + cross
            on = o * lax.rsqrt(jnp.mean(o * o, axis=-1, keepdims=True) + EPS) * og_ref[:, vv]
            zz = z_ref[rows, vv].astype(F32)
            o_ref[rows, vv] = (on * (zz * _sigmoid(zz))).astype(BF16)
        return 0

    lax.fori_loop(0, S // RET_CHUNK, chunk, 0)


def _retention(P, cosf, sinf, dec, qd, kd, chunk_decay, qg, kg, og, B, S):
    const2 = lambda b: (0, 0)
    const3 = lambda b: (0, 0, 0)
    return pl.pallas_call(
        functools.partial(_ret_kernel, chunk_decay),
        out_shape=jax.ShapeDtypeStruct((B * S, RET_V_W), BF16),
        grid=(B,),
        in_specs=[
            pl.BlockSpec((S, RET_QK_W), lambda b: (b, P_RQ)),
            pl.BlockSpec((S, RET_QK_W), lambda b: (b, P_RK)),
            pl.BlockSpec((S, RET_V_W), lambda b: (b, P_RV)),
            pl.BlockSpec((S, RET_V_W), lambda b: (b, P_RZ)),
            pl.BlockSpec((S, RET_QK_DIM), const2),
            pl.BlockSpec((S, RET_QK_DIM), const2),
            pl.BlockSpec((RET_HEADS, RET_CHUNK, RET_CHUNK), const3),
            pl.BlockSpec((RET_HEADS, RET_CHUNK, RET_QK_DIM), const3),
            pl.BlockSpec((RET_HEADS, RET_CHUNK, RET_QK_DIM), const3),
            pl.BlockSpec((1, RET_QK_DIM), const2),
            pl.BlockSpec((1, RET_QK_DIM), const2),
            pl.BlockSpec((1, RET_V_W), const2),
        ],
        out_specs=pl.BlockSpec((S, RET_V_W), lambda b: (b, 0)),
        scratch_shapes=[pltpu.VMEM((RET_HEADS, RET_QK_DIM, RET_V_DIM), F32)],
        compiler_params=pltpu.CompilerParams(
            dimension_semantics=("arbitrary",), vmem_limit_bytes=VMEM_LIMIT),
        name="retention",
    )(P, P, P, P, cosf, sinf, dec, qd, kd, qg, kg, og)


def _sb_kernel(tab_ref, q_ref, k_ref, v_ref, z_ref, qg_ref, kg_ref, tri_ref, o_ref,
               qlo_scr, qhi_scr, kn_scr, vlo_scr, vhi_scr, hl_scr, s_scr, w_scr, carry_scr, acc_scr):
    S = q_ref.shape[0]
    nc = min(256, S)
    lane_n = lax.broadcasted_iota(jnp.int32, (nc, 2 * SB_HEAD_DIM), 1)
    lo_n = lane_n < SB_HEAD_DIM
    inv_d = 1.0 / SB_HEAD_DIM

    def head_rms(ref, rows, g_ref):
        x = ref[rows, :].astype(F32)
        xx = x * x
        s_all = jnp.sum(xx, axis=-1, keepdims=True)
        s_lo = jnp.sum(jnp.where(lo_n, xx, 0.0), axis=-1, keepdims=True)
        ms = jnp.where(lo_n, s_lo, s_all - s_lo) * inv_d
        return x * lax.rsqrt(ms + EPS) * g_ref[...]

    def norm_body(i, _):
        rows = pl.ds(pl.multiple_of(i * nc, nc), nc)
        qn = head_rms(q_ref, rows, qg_ref) * (SB_HEAD_DIM ** -0.5 * LOG2E)
        qlo_scr[rows, :] = jnp.where(lo_n, qn, 0.0).astype(BF16)
        qhi_scr[rows, :] = jnp.where(lo_n, 0.0, qn).astype(BF16)
        kn_scr[rows, :] = head_rms(k_ref, rows, kg_ref).astype(BF16)
        v = v_ref[rows, :].astype(F32)
        vlo_scr[rows, :] = jnp.where(lo_n, v, 0.0).astype(BF16)
        vhi_scr[rows, :] = jnp.where(lo_n, 0.0, v).astype(BF16)
        return 0

    lax.fori_loop(0, S // nc, norm_body, 0)

    tile = hl_scr.shape[2]
    n_tiles = S // tile
    n_sub = tile // SB_LANES
    n_diag = n_tiles
    n_off = n_tiles * (n_tiles - 1) // 2
    n_tab = n_diag + n_off
    qi_idx = lax.broadcasted_iota(jnp.int32, (tile, tile), 0)
    ki_idx = lax.broadcasted_iota(jnp.int32, (tile, tile), 1)
    causal = ki_idx < qi_idx

    def tile_rows(idx):
        return pl.ds(pl.multiple_of(idx * tile, tile), tile)

    def stage_scores(n, slot, masked):
        k2 = kn_scr[tile_rows(tab_ref[n_tab + n]), :]
        qrows = tile_rows(tab_ref[n])
        for hh, q_scr in enumerate((qlo_scr, qhi_scr)):
            z = _nt_dot(q_scr[qrows, :], k2)
            neg_abs = lax.bitcast_convert_type(
                lax.bitcast_convert_type(z, jnp.uint32) | jnp.uint32(0x80000000), F32)
            s = jnp.minimum(z, 0.0) - jnp.log(1.0 + jnp.exp2(neg_abs)) * LOG2E
            lk = s - z
            if masked:
                lk = jnp.where(causal, lk, 0.0)
                s = jnp.where(causal, s, SB_MASKED_LOGIT)
            hi = lk.astype(BF16)
            lo = (lk - hi.astype(F32)).astype(BF16)
            parts = []
            for sb in range(n_sub):
                ks = slice(sb * SB_LANES, (sb + 1) * SB_LANES)
                parts += [hi[:, ks], lo[:, ks]]
            hl_scr[slot, hh] = jnp.concatenate(parts, axis=1)
            s_scr[slot, hh] = s

    def stage_weights(n, slot, first):
        qi = tab_ref[n]
        for hh in range(2):
            carry = jnp.zeros((tile, SB_LANES), F32) if first else carry_scr[qi, hh]
            for sb in reversed(range(n_sub)):
                cs = _dot(hl_scr[slot, hh, :, 2 * sb * SB_LANES:2 * (sb + 1) * SB_LANES], tri_ref[...])
                after = cs[:, :SB_LANES] + carry
                carry = carry + cs[:, SB_LANES:]
                w = jnp.exp2(s_scr[slot, hh, :, sb * SB_LANES:(sb + 1) * SB_LANES] + after)
                w_scr[slot, :, hh * tile + sb * SB_LANES:hh * tile + (sb + 1) * SB_LANES] = w.astype(BF16)
            carry_scr[qi, hh] = carry

    def stage_values(n, slot, first):
        krows = tile_rows(tab_ref[n_tab + n])
        qi = tab_ref[n]
        vcat = jnp.concatenate([vlo_scr[krows, :], vhi_scr[krows, :]], axis=0)
        contrib = _dot(w_scr[slot], vcat)
        if first:
            acc_scr[qi] = contrib
        else:
            acc_scr[qi] += contrib

    def run_pipelined(n0, count, diag):
        def step(it, parity, a, b, c):
            if a:
                stage_scores(n0 + it, parity, diag)
            if b:
                stage_weights(n0 + it - 1, 1 - parity, diag)
            if c:
                stage_values(n0 + it - 2, parity, diag)

        def static_step(it):
            step(it, it % 2, 0 <= it < count, 0 <= it - 1 < count, 0 <= it - 2 < count)

        steady = list(range(2, count))
        for it in (0, 1):
            static_step(it)
        if len(steady) % 2:
            static_step(steady.pop(0))
        if steady:
            first = steady[0]

            def pair(p, _):
                for u in range(2):
                    step(first + 2 * p + u, (first + u) % 2, True, True, True)
                return 0

            lax.fori_loop(0, len(steady) // 2, pair, 0)
        for it in sorted({count, count + 1} - {0, 1}):
            static_step(it)

    run_pipelined(0, n_diag, True)
    run_pipelined(n_diag, n_off, False)

    def finish(qi, _):
        qrows = tile_rows(qi)
        zz = z_ref[qrows, :].astype(F32)
        o_ref[qrows, :] = (acc_scr[qi] * (zz * _sigmoid(zz))).astype(BF16)
        return 0

    lax.fori_loop(0, n_tiles, finish, 0)


def _sb_tile_table(n_tiles):
    tiles = [(i, i) for i in range(n_tiles)]
    tiles += [(i, j) for i in range(1, n_tiles) for j in range(i - 1, -1, -1)]
    return jnp.asarray([q for q, _ in tiles] + [k for _, k in tiles], dtype=jnp.int32)


def _stick_breaking(P, qg, kg, tri, B, S):
    n_pairs = SB_HEADS // 2
    pw = 2 * SB_HEAD_DIM
    tile = min(SB_TILE, S)
    n_tiles = S // tile
    const2 = lambda b, p, tab: (0, 0)
    return pl.pallas_call(
        _sb_kernel,
        out_shape=jax.ShapeDtypeStruct((B * S, SB_W), BF16),
        grid_spec=pltpu.PrefetchScalarGridSpec(
            num_scalar_prefetch=1,
            grid=(B, n_pairs),
            in_specs=[
                pl.BlockSpec((S, pw), lambda b, p, tab: (b, P_SQ + p)),
                pl.BlockSpec((S, pw), lambda b, p, tab: (b, P_SK + p)),
                pl.BlockSpec((S, pw), lambda b, p, tab: (b, P_SV + p)),
                pl.BlockSpec((S, pw), lambda b, p, tab: (b, P_SZ + p)),
                pl.BlockSpec((1, pw), const2),
                pl.BlockSpec((1, pw), const2),
                pl.BlockSpec((2 * SB_LANES, 2 * SB_LANES), const2),
            ],
            out_specs=pl.BlockSpec((S, pw), lambda b, p, tab: (b, p)),
            scratch_shapes=[pltpu.VMEM((S, pw), BF16) for _ in range(5)] + [
                pltpu.VMEM((2, 2, tile, 2 * tile), BF16),
                pltpu.VMEM((2, 2, tile, tile), F32),
                pltpu.VMEM((2, tile, 2 * tile), BF16),
                pltpu.VMEM((n_tiles, 2, tile, SB_LANES), F32),
                pltpu.VMEM((n_tiles, tile, pw), F32),
            ]),
        compiler_params=pltpu.CompilerParams(
            dimension_semantics=("arbitrary", "arbitrary"), vmem_limit_bytes=VMEM_LIMIT),
        name="stickbreak",
    )(_sb_tile_table(n_tiles), P, P, P, P, qg, kg, tri)


def _ssm_kernel(u_ref, are_ref, aim_ref, bre_ref, bim_ref, cre_ref, cimn_ref, d_ref, wg_ref, bg_ref,
                o_ref, bur_scr, bui_scr, hr_scr, hi_scr, cr_scr, ci_scr):
    T, B, _ = u_ref.shape
    rows_n = T * B
    half_k = SSM_W // 2
    half_n = SSM_STATES // 2

    @pl.when(pl.program_id(0) == 0)
    def _():
        cr_scr[...] = jnp.zeros_like(cr_scr)
        ci_scr[...] = jnp.zeros_like(ci_scr)

    u = u_ref[:, :, :SSM_W].reshape(rows_n, SSM_W)
    for c in range(2):
        uc = u[:, c * half_k:(c + 1) * half_k]
        bur_scr[:, c * half_n:(c + 1) * half_n] = _dot(uc, bre_ref[c])
        bui_scr[:, c * half_n:(c + 1) * half_n] = _dot(uc, bim_ref[c])

    for lc in range(SSM_STATES // SSM_LANE_CHUNK):
        ls = slice(lc * SSM_LANE_CHUNK, (lc + 1) * SSM_LANE_CHUNK)
        ar = jnp.broadcast_to(are_ref[:, ls], (B, SSM_LANE_CHUNK))
        ai = jnp.broadcast_to(aim_ref[:, ls], (B, SSM_LANE_CHUNK))

        def step(t, carry):
            hr, hi = carry
            rows = pl.ds(pl.multiple_of(t * B, B), B)
            nr = ar * hr - ai * hi + bur_scr[rows, ls]
            ni = ar * hi + ai * hr + bui_scr[rows, ls]
            hr_scr[rows, ls] = nr.astype(BF16)
            hi_scr[rows, ls] = ni.astype(BF16)
            return nr, ni

        hr, hi = lax.fori_loop(0, T, step, (cr_scr[:, ls], ci_scr[:, ls]), unroll=8)
        cr_scr[:, ls] = hr
        ci_scr[:, ls] = hi

    ys = []
    for c in range(2):
        ss = slice(c * half_n, (c + 1) * half_n)
        ys.append(_dot(hr_scr[:, ss], cre_ref[c]) + _dot(hi_scr[:, ss], cimn_ref[c]))
    y = jnp.concatenate(ys, axis=1) + d_ref[...] * u.astype(F32)
    y = 0.5 * y * (1.0 + jnp.tanh(math.sqrt(2.0 / math.pi) * (y + 0.044715 * (y * y * y))))
    y = y * _sigmoid(_dot(y.astype(BF16), wg_ref[...]) + bg_ref[...])
    cz = u_ref[:, :, SSM_W:].reshape(rows_n, SSM_W).astype(F32)
    o_ref[...] = (y * (cz * _sigmoid(cz))).astype(BF16).reshape(T, B, SSM_W)


def _ssm(U3, are, aim, bre, bim, cre, cimn, d, wg, bg, B, S):
    T = min(SSM_T, S)
    rows_n = T * B
    const2 = lambda t: (0, 0)
    const3 = lambda t: (0, 0, 0)
    return pl.pallas_call(
        _ssm_kernel,
        out_shape=jax.ShapeDtypeStruct((S, B, SSM_W), BF16),
        grid=(S // T,),
        in_specs=[
            pl.BlockSpec((T, B, COL_TILE), lambda t: (t, 0, 0)),
            pl.BlockSpec((1, SSM_STATES), const2),
            pl.BlockSpec((1, SSM_STATES), const2),
            pl.BlockSpec((2, SSM_W // 2, SSM_STATES // 2), const3),
            pl.BlockSpec((2, SSM_W // 2, SSM_STATES // 2), const3),
            pl.BlockSpec((2, SSM_STATES // 2, SSM_W // 2), const3),
            pl.BlockSpec((2, SSM_STATES // 2, SSM_W // 2), const3),
            pl.BlockSpec((1, SSM_W), const2),
            pl.BlockSpec((SSM_W, SSM_W), const2),
            pl.BlockSpec((1, SSM_W), const2),
        ],
        out_specs=pl.BlockSpec((T, B, SSM_W), lambda t: (t, 0, 0)),
        scratch_shapes=[
            pltpu.VMEM((rows_n, SSM_STATES), F32), pltpu.VMEM((rows_n, SSM_STATES), F32),
            pltpu.VMEM((rows_n, SSM_STATES), BF16), pltpu.VMEM((rows_n, SSM_STATES), BF16),
            pltpu.VMEM((B, SSM_STATES), F32), pltpu.VMEM((B, SSM_STATES), F32),
        ],
        compiler_params=pltpu.CompilerParams(
            dimension_semantics=("arbitrary",), vmem_limit_bytes=VMEM_LIMIT),
        name="ssm",
    )(U3, are, aim, bre, bim, cre, cimn, d, wg, bg)


def _merge_kernel(ya_ref, yb_ref, yc_ref, ga_ref, gb_ref, gc_ref, x_ref, pa_ref, pb_ref, pc_ref, wo_ref,
                  o_ref):
    tm = x_ref.shape[0]
    mc = min(512, tm)
    for r in range(0, tm, mc):
        rows = slice(r, r + mc)
        m = _sigmoid(ga_ref[rows, :].astype(F32)) * _dot(ya_ref[rows, :], pa_ref[...])
        m += _sigmoid(gb_ref[rows, :].astype(F32)) * _dot(yb_ref[rows, :], pb_ref[...])
        m += _sigmoid(gc_ref[rows, :].astype(F32)) * _dot(yc_ref[rows, :], pc_ref[...])
        o_ref[rows, :] = x_ref[rows, :] + _dot(m.astype(BF16), wo_ref[...])


def _merge(ya, yb, yc2d, P, x2d, pa, pb, pc, wo, B, S):
    tm = min(1024, S)
    nt = S // tm
    tok = lambda b, t: (b * nt + t, 0)
    const2 = lambda b, t: (0, 0)
    return pl.pallas_call(
        _merge_kernel,
        out_shape=jax.ShapeDtypeStruct((B * S, D_MODEL), F32),
        grid=(B, nt),
        in_specs=[
            pl.BlockSpec((tm, RET_V_W), tok),
            pl.BlockSpec((tm, SB_W), tok),
            pl.BlockSpec((tm, SSM_W), lambda b, t: (t, b)),
            pl.BlockSpec((tm, D_MODEL), lambda b, t: (b * nt + t, P_GA)),
            pl.BlockSpec((tm, D_MODEL), lambda b, t: (b * nt + t, P_GB)),
            pl.BlockSpec((tm, D_MODEL), lambda b, t: (b * nt + t, P_GC)),
            pl.BlockSpec((tm, D_MODEL), tok),
            pl.BlockSpec((RET_V_W, D_MODEL), const2),
            pl.BlockSpec((SB_W, D_MODEL), const2),
            pl.BlockSpec((SSM_W, D_MODEL), const2),
            pl.BlockSpec((D_MODEL, D_MODEL), const2),
        ],
        out_specs=pl.BlockSpec((tm, D_MODEL), tok),
        compiler_params=pltpu.CompilerParams(
            dimension_semantics=("arbitrary", "arbitrary"), vmem_limit_bytes=VMEM_LIMIT),
        name="merge",
    )(ya, yb, yc2d, P, P, P, x2d, pa, pb, pc, wo)


def _rope_tables(S):
    half = RET_QK_DIM // 2
    inv_freq = ROPE_BASE ** (-jnp.arange(half, dtype=F32) / half)
    ang = jnp.arange(S, dtype=F32)[:, None] * inv_freq[None, :]
    cos, sin = jnp.cos(ang), jnp.sin(ang)
    return jnp.concatenate([cos, cos], axis=1), jnp.concatenate([-sin, sin], axis=1)


def _decay_tables():
    log_gamma = jnp.log1p(-jnp.exp2(-5.0 - jnp.arange(RET_HEADS, dtype=F32)))
    idx = jnp.arange(RET_CHUNK, dtype=F32)
    rel = idx[:, None] - idx[None, :]
    dec = jnp.where(rel >= 0, jnp.exp(jnp.maximum(rel, 0.0)[None] * log_gamma[:, None, None]), 0.0)
    kdec = jnp.exp((RET_CHUNK - 1 - idx)[None, :] * log_gamma[:, None])
    qdec = jnp.exp((idx + 1)[None, :] * log_gamma[:, None])
    full = lambda a: jnp.broadcast_to(a[:, :, None], (RET_HEADS, RET_CHUNK, RET_QK_DIM))
    chunk_decay = tuple(float(math.exp(RET_CHUNK * math.log1p(-2.0 ** (-5.0 - h)))) for h in range(RET_HEADS))
    return dec, full(qdec), full(kdec), chunk_decay


def _ssm_tables(a_re, a_im, log_dt, b_re, b_im, c_re, c_im):
    dt = jnp.exp(log_dt)[:, None]
    mag = jnp.exp(dt * a_re)
    ab_re = mag * jnp.cos(dt * a_im)
    ab_im = mag * jnp.sin(dt * a_im)
    den = a_re * a_re + a_im * a_im
    nr = ab_re - 1.0
    coef_re = (nr * a_re + ab_im * a_im) / den
    coef_im = (ab_im * a_re - nr * a_im) / den
    bb_re = coef_re[..., None] * b_re - coef_im[..., None] * b_im
    bb_im = coef_re[..., None] * b_im + coef_im[..., None] * b_re
    gl = SSM_GROUPS // 2
    eye = jnp.eye(gl, dtype=F32)

    def b_blocks(bb):
        return jnp.einsum('cgpm,gh->cgmhp', bb.reshape(2, gl, SSM_STATE, SSM_GROUP), eye).reshape(
            2, gl * SSM_GROUP, gl * SSM_STATE).astype(BF16)

    def c_blocks(cc):
        return jnp.einsum('cgmp,gh->cgphm', cc.reshape(2, gl, SSM_GROUP, SSM_STATE), eye).reshape(
            2, gl * SSM_STATE, gl * SSM_GROUP).astype(BF16)

    return (ab_re.reshape(1, SSM_STATES), ab_im.reshape(1, SSM_STATES),
            b_blocks(bb_re), b_blocks(bb_im), c_blocks(c_re), c_blocks(-c_im))


def _suffix_sum_matrix():
    k = jnp.arange(SB_LANES)
    upper = (k[:, None] > k[None, :]).astype(BF16)
    half = jnp.concatenate([upper, jnp.ones((SB_LANES, SB_LANES), BF16)], axis=1)
    return jnp.concatenate([half, half], axis=0)


def kernel(x, norm_g, w_in, ret_q_norm, ret_k_norm, ret_out_norm, sb_q_norm, sb_k_norm, ssm_a_re, ssm_a_im,
           ssm_log_dt, ssm_b_re, ssm_b_im, ssm_c_re, ssm_c_im, ssm_d, ssm_w_glu, ssm_b_glu, proj_a, proj_b,
           proj_c, w_out):
    B, S, D = x.shape
    depth = w_in.shape[0]
    assert D == D_MODEL and B == 16 and S % 128 == 0
    cosf, sinf = _rope_tables(S)
    dec, qd, kd, chunk_decay = _decay_tables()
    tri = _suffix_sum_matrix()
    x2d = x.reshape(B * S, D)
    for l in range(depth):
        P, U = _inproj(x2d, norm_g[l].reshape(1, D), w_in[l].astype(BF16), B, S)
        ya = _retention(P, cosf, sinf, dec, qd, kd, chunk_decay,
                        ret_q_norm[l].reshape(1, -1),
                        (ret_k_norm[l] * (RET_QK_DIM ** -0.5)).reshape(1, -1),
                        ret_out_norm[l].reshape(1, -1), B, S)
        yb = _stick_breaking(P, jnp.tile(sb_q_norm[l], 2).reshape(1, -1),
                             jnp.tile(sb_k_norm[l], 2).reshape(1, -1), tri, B, S)
        are, aim, bre, bim, cre, cimn = _ssm_tables(ssm_a_re[l], ssm_a_im[l], ssm_log_dt[l], ssm_b_re[l],
                                                    ssm_b_im[l], ssm_c_re[l], ssm_c_im[l])
        yc = _ssm(U.reshape(S, B, COL_TILE), are, aim, bre, bim, cre, cimn,
                  ssm_d[l].reshape(1, -1), ssm_w_glu[l].astype(BF16), ssm_b_glu[l].reshape(1, -1), B, S)
        x2d = _merge(ya, yb, yc.reshape(S, B * SSM_W), P, x2d, proj_a[l].astype(BF16),
                     proj_b[l].astype(BF16), proj_c[l].astype(BF16), w_out[l].astype(BF16), B, S)
    return x2d.reshape(B, S, D)
```

```python
import functools
import math

import jax
import jax.numpy as jnp
from jax import lax
from jax.experimental import pallas as pl
from jax.experimental.pallas import tpu as pltpu

F32 = jnp.float32
BF16 = jnp.bfloat16

D_MODEL = 1024
RET_HEADS = 4
RET_QK_W = 512
RET_V_W = 1024
RET_QK_DIM = 128
RET_V_DIM = 256
RET_CHUNK = 128
ROPE_BASE = 10000.0
SB_HEADS = 8
SB_W = 512
SB_HEAD_DIM = 64
SB_TILE = 256
SB_LANES = 128
SB_MASKED_LOGIT = -1e30
SB_MAX_LOGIT = 126.0
LOG2E = math.log2(math.e)
SSM_W = 512
SSM_GROUP = 16
SSM_GROUPS = 32
SSM_STATE = 64
SSM_STATES = SSM_GROUPS * SSM_STATE
EPS = 1e-6
IN_W = 9216

COL_TILE = 1024
SSM_COL_TILE = 5
P_W = IN_W - COL_TILE
P_RQ, P_RK = 0, 1
P_RV, P_RZ = 1, 2
P_SQ, P_SK, P_SV, P_SZ = 24, 28, 32, 36
P_GA, P_GB, P_GC = 5, 6, 7

SSM_T = 32
SSM_LANE_CHUNK = 512
VMEM_LIMIT = 56 * 1024 * 1024


def _sigmoid(v):
    return 1.0 / (1.0 + jnp.exp(-v))


def _nt_dot(a, b):
    return lax.dot_general(a, b, (((1,), (1,)), ((), ())), preferred_element_type=F32)


def _dot(a, b):
    return jnp.dot(a, b, preferred_element_type=F32)


def _inproj_kernel(x_ref, g_ref, w_ref, p_ref, u_ref, h_scr):
    j = pl.program_id(2)
    tm = x_ref.shape[0]
    rc = min(256, tm)

    @pl.when(j == 0)
    def _():
        def body(i, _):
            rows = pl.ds(pl.multiple_of(i * rc, rc), rc)
            x = x_ref[rows, :]
            ms = jnp.mean(x * x, axis=-1, keepdims=True)
            h_scr[rows, :] = (x * lax.rsqrt(ms + EPS) * g_ref[...]).astype(BF16)
            return 0
        lax.fori_loop(0, tm // rc, body, 0)

    mc = min(512, tm)

    @pl.when(j != SSM_COL_TILE)
    def _():
        for r in range(0, tm, mc):
            p_ref[r:r + mc, :] = _dot(h_scr[r:r + mc, :], w_ref[...]).astype(BF16)

    @pl.when(j == SSM_COL_TILE)
    def _():
        for r in range(0, tm, mc):
            u_ref[r:r + mc, :] = _dot(h_scr[r:r + mc, :], w_ref[...]).astype(BF16)


def _inproj(x2d, g, w, B, S):
    tm = min(1024, S)
    nt = S // tm
    n_col = IN_W // COL_TILE
    return pl.pallas_call(
        _inproj_kernel,
        out_shape=(jax.ShapeDtypeStruct((B * S, P_W), BF16),
                   jax.ShapeDtypeStruct((S, B * COL_TILE), BF16)),
        grid=(B, nt, n_col),
        in_specs=[
            pl.BlockSpec((tm, D_MODEL), lambda b, t, j: (b * nt + t, 0)),
            pl.BlockSpec((1, D_MODEL), lambda b, t, j: (0, 0)),
            pl.BlockSpec((D_MODEL, COL_TILE), lambda b, t, j: (0, j)),
        ],
        out_specs=(
            pl.BlockSpec((tm, COL_TILE),
                         lambda b, t, j: (b * nt + t, j - (j >= SSM_COL_TILE).astype(jnp.int32))),
            pl.BlockSpec((tm, COL_TILE), lambda b, t, j: (t, b)),
        ),
        scratch_shapes=[pltpu.VMEM((tm, D_MODEL), BF16)],
        compiler_params=pltpu.CompilerParams(
            dimension_semantics=("arbitrary", "arbitrary", "arbitrary"),
            vmem_limit_bytes=VMEM_LIMIT),
        name="inproj",
    )(x2d, g, w)


def _ret_kernel(chunk_decay, q_ref, k_ref, v_ref, z_ref, cos_ref, sin_ref, dec_ref, qd_ref, kd_ref,
                qg_ref, kg_ref, og_ref, perm_ref, avg_ref, eye_ref, o_ref,
                qr_scr, qdec_scr, kr_scr, kdt_scr, st_scr, stb_scr, sd_scr, cr_scr):
    S = q_ref.shape[0]
    n_chunks = S // RET_CHUNK
    heads = range(RET_HEADS)
    qk = [slice(h * RET_QK_DIM, (h + 1) * RET_QK_DIM) for h in heads]
    vv = [slice(h * RET_V_DIM, (h + 1) * RET_V_DIM) for h in heads]

    def chunk_rows(c):
        return pl.ds(pl.multiple_of(c * RET_CHUNK, RET_CHUNK), RET_CHUNK)

    def prep(c, _):
        rows = chunk_rows(c)
        cosf = cos_ref[rows, :]
        sinf = sin_ref[rows, :]
        gains = ((qg_ref[0:1, :] * cosf, qg_ref[1:2, :] * sinf), (kg_ref[0:1, :] * cosf, kg_ref[1:2, :] * sinf))
        raw = [[ref[rows, qk[h]] for h in heads] for ref in (q_ref, k_ref)]
        rolled = [[_dot(t, perm_ref[...]) for t in ts] for ts in raw]
        f32 = [[t.astype(F32) for t in ts] for ts in raw]
        mean_sq = [[_dot((t * t).astype(BF16), avg_ref[...]) for t in ts] for ts in f32]
        rot = [[(f32[i][h] * gains[i][0] + rolled[i][h] * gains[i][1]) * lax.rsqrt(mean_sq[i][h] + EPS)
                for h in heads] for i in range(2)]
        kdec = [(rot[1][h] * kd_ref[h]).astype(BF16) for h in heads]
        kdec_t = [_nt_dot(eye_ref[...], t) for t in kdec]
        for h in heads:
            qr_scr[rows, qk[h]] = rot[0][h].astype(BF16)
            qdec_scr[rows, qk[h]] = (rot[0][h] * qd_ref[h]).astype(BF16)
            kr_scr[rows, qk[h]] = rot[1][h].astype(BF16)
            kdt_scr[c, h] = kdec_t[h].astype(BF16)
        return 0

    lax.fori_loop(0, n_chunks, prep, 0)

    st_scr[...] = jnp.zeros_like(st_scr)
    stb_scr[...] = jnp.zeros_like(stb_scr)

    def state_matmuls(c):
        rows = chunk_rows(c)
        scores = [_nt_dot(qr_scr[rows, qk[h]], kr_scr[rows, qk[h]]) for h in heads]
        cross = [_dot(qdec_scr[rows, qk[h]], stb_scr[h]) for h in heads]
        kv = [_dot(kdt_scr[c, h], v_ref[rows, vv[h]]) for h in heads]
        return scores, cross, kv

    def state_update(slot, scores, cross, kv):
        for h in heads:
            sd_scr[slot, h] = (scores[h] * dec_ref[h]).astype(BF16)
            cr_scr[slot, h] = cross[h]
            st = st_scr[h] * chunk_decay[h] + kv[h]
            st_scr[h] = st
            stb_scr[h] = st.astype(BF16)

    def inner_matmuls(c, slot):
        rows = chunk_rows(c)
        return [_dot(sd_scr[slot, h], v_ref[rows, vv[h]]) for h in heads]

    def write_out(c, slot, inner):
        rows = chunk_rows(c)
        for h in heads:
            o = inner[h] + cr_scr[slot, h]
            on = o * lax.rsqrt(jnp.mean(o * o, axis=-1, keepdims=True) + EPS) * og_ref[:, vv[h]]
            zz = z_ref[rows, vv[h]].astype(F32)
            o_ref[rows, vv[h]] = (on * (zz * _sigmoid(zz))).astype(BF16)

    state_update(0, *state_matmuls(0))

    def step(c, _):
        slot = c % 2
        sm = state_matmuls(c)
        inner = inner_matmuls(c - 1, 1 - slot)
        state_update(slot, *sm)
        write_out(c - 1, 1 - slot, inner)
        return 0

    lax.fori_loop(1, n_chunks, step, 0)
    last = n_chunks - 1
    write_out(last, last % 2, inner_matmuls(last, last % 2))


def _retention(P, cosf, sinf, dec, qd, kd, chunk_decay, qg, kg, og, B, S):
    half = RET_QK_DIM // 2
    lane = jnp.arange(RET_QK_DIM)
    perm = (lane[:, None] == (lane[None, :] + half) % RET_QK_DIM).astype(BF16)
    avg = jnp.full((RET_QK_DIM, RET_QK_DIM), 1.0 / RET_QK_DIM, BF16)
    eye = jnp.eye(RET_QK_DIM, dtype=BF16)
    with_rolled = lambda g: jnp.stack([g, jnp.roll(g, half)])
    const2 = lambda b: (0, 0)
    const3 = lambda b: (0, 0, 0)
    return pl.pallas_call(
        functools.partial(_ret_kernel, chunk_decay),
        out_shape=jax.ShapeDtypeStruct((B * S, RET_V_W), BF16),
        grid=(B,),
        in_specs=[
            pl.BlockSpec((S, RET_QK_W), lambda b: (b, P_RQ)),
            pl.BlockSpec((S, RET_QK_W), lambda b: (b, P_RK)),
            pl.BlockSpec((S, RET_V_W), lambda b: (b, P_RV)),
            pl.BlockSpec((S, RET_V_W), lambda b: (b, P_RZ)),
            pl.BlockSpec((S, RET_QK_DIM), const2),
            pl.BlockSpec((S, RET_QK_DIM), const2),
            pl.BlockSpec((RET_HEADS, RET_CHUNK, RET_CHUNK), const3),
            pl.BlockSpec((RET_HEADS, RET_CHUNK, RET_QK_DIM), const3),
            pl.BlockSpec((RET_HEADS, RET_CHUNK, RET_QK_DIM), const3),
            pl.BlockSpec((2, RET_QK_DIM), const2),
            pl.BlockSpec((2, RET_QK_DIM), const2),
            pl.BlockSpec((1, RET_V_W), const2),
            pl.BlockSpec((RET_QK_DIM, RET_QK_DIM), const2),
            pl.BlockSpec((RET_QK_DIM, RET_QK_DIM), const2),
            pl.BlockSpec((RET_QK_DIM, RET_QK_DIM), const2),
        ],
        out_specs=pl.BlockSpec((S, RET_V_W), lambda b: (b, 0)),
        scratch_shapes=[
            pltpu.VMEM((S, RET_QK_W), BF16),
            pltpu.VMEM((S, RET_QK_W), BF16),
            pltpu.VMEM((S, RET_QK_W), BF16),
            pltpu.VMEM((S // RET_CHUNK, RET_HEADS, RET_QK_DIM, RET_CHUNK), BF16),
            pltpu.VMEM((RET_HEADS, RET_QK_DIM, RET_V_DIM), F32),
            pltpu.VMEM((RET_HEADS, RET_QK_DIM, RET_V_DIM), BF16),
            pltpu.VMEM((2, RET_HEADS, RET_CHUNK, RET_CHUNK), BF16),
            pltpu.VMEM((2, RET_HEADS, RET_CHUNK, RET_V_DIM), F32),
        ],
        compiler_params=pltpu.CompilerParams(
            dimension_semantics=("arbitrary",), vmem_limit_bytes=VMEM_LIMIT),
        name="retention",
    )(P, P, P, P, cosf, sinf, dec, qd, kd, with_rolled(qg), with_rolled(kg), og, perm, avg, eye)


def _sb_kernel(tab_ref, q_ref, k_ref, v_ref, z_ref, qg_ref, kg_ref, tri_ref, avg_ref, o_ref,
               qlo_scr, qhi_scr, kn_scr, vlo_scr, vhi_scr, nlk_scr, s_scr, rs_scr, w_scr, carry_scr, acc_scr):
    S = q_ref.shape[0]
    nc = min(256, S)
    lane_n = lax.broadcasted_iota(jnp.int32, (nc, 2 * SB_HEAD_DIM), 1)
    lo_n = lane_n < SB_HEAD_DIM

    def head_rms(ref, rows, g_ref):
        x = ref[rows, :].astype(F32)
        ms = _dot((x * x).astype(BF16), avg_ref[...])
        return x * lax.rsqrt(ms + EPS) * g_ref[...]

    def norm_body(i, _):
        rows = pl.ds(pl.multiple_of(i * nc, nc), nc)
        qn = head_rms(q_ref, rows, qg_ref) * (SB_HEAD_DIM ** -0.5 * LOG2E)
        qlo_scr[rows, :] = jnp.where(lo_n, qn, 0.0).astype(BF16)
        qhi_scr[rows, :] = jnp.where(lo_n, 0.0, qn).astype(BF16)
        kn_scr[rows, :] = head_rms(k_ref, rows, kg_ref).astype(BF16)
        v = v_ref[rows, :].astype(F32)
        vlo_scr[rows, :] = jnp.where(lo_n, v, 0.0).astype(BF16)
        vhi_scr[rows, :] = jnp.where(lo_n, 0.0, v).astype(BF16)
        return 0

    lax.fori_loop(0, S // nc, norm_body, 0)

    tile = nlk_scr.shape[2]
    n_tiles = S // tile
    n_sub = tile // SB_LANES
    n_diag = n_tiles
    n_off = n_tiles * (n_tiles - 1) // 2
    n_tab = n_diag + n_off
    qi_idx = lax.broadcasted_iota(jnp.int32, (tile, tile), 0)
    ki_idx = lax.broadcasted_iota(jnp.int32, (tile, tile), 1)
    causal = ki_idx < qi_idx

    def tile_rows(idx):
        return pl.ds(pl.multiple_of(idx * tile, tile), tile)

    def stage_scores(n, slot, masked):
        k2 = kn_scr[tile_rows(tab_ref[n_tab + n]), :]
        qrows = tile_rows(tab_ref[n])
        for hh, q_scr in enumerate((qlo_scr, qhi_scr)):
            z = jnp.minimum(_nt_dot(q_scr[qrows, :], k2), SB_MAX_LOGIT)
            nlk = jnp.log(1.0 + jnp.exp2(z)) * LOG2E
            s = z - nlk
            if masked:
                nlk = jnp.where(causal, nlk, 0.0)
                s = jnp.where(causal, s, SB_MASKED_LOGIT)
            nlk_scr[slot, hh] = nlk.astype(BF16)
            s_scr[slot, hh] = s
            rs_scr[slot, hh] = jnp.broadcast_to(jnp.sum(nlk, axis=-1, keepdims=True), (tile, SB_LANES))

    def stage_weights(n, slot, first):
        qi = tab_ref[n]
        for hh in range(2):
            carry = jnp.zeros((tile, SB_LANES), F32) if first else carry_scr[qi, hh]
            after = _dot(nlk_scr[slot, hh], tri_ref[...]) + jnp.concatenate([carry] * n_sub, axis=1)
            w = jnp.exp2(s_scr[slot, hh] + after)
            w_scr[slot, :, hh * tile:(hh + 1) * tile] = w.astype(BF16)
            carry_scr[qi, hh] = carry - rs_scr[slot, hh]

    def stage_values(n, slot, first):
        krows = tile_rows(tab_ref[n_tab + n])
        qi = tab_ref[n]
        vcat = jnp.concatenate([vlo_scr[krows, :], vhi_scr[krows, :]], axis=0)
        contrib = _dot(w_scr[slot], vcat)
        if first:
            acc_scr[qi] = contrib
        else:
            acc_scr[qi] += contrib

    def run_pipelined(n0, count, diag):
        def step(it, parity, a, b, c):
            if a:
                stage_scores(n0 + it, parity, diag)
            if b:
                stage_weights(n0 + it - 1, 1 - parity, diag)
            if c:
                stage_values(n0 + it - 2, parity, diag)

        def static_step(it):
            step(it, it % 2, 0 <= it < count, 0 <= it - 1 < count, 0 <= it - 2 < count)

        steady = list(range(2, count))
        for it in (0, 1):
            static_step(it)
        if len(steady) % 2:
            static_step(steady.pop(0))
        if steady:
            first = steady[0]

            def pair(p, _):
                for u in range(2):
                    step(first + 2 * p + u, (first + u) % 2, True, True, True)
                return 0

            lax.fori_loop(0, len(steady) // 2, pair, 0)
        for it in sorted({count, count + 1} - {0, 1}):
            static_step(it)

    run_pipelined(0, n_diag, True)
    run_pipelined(n_diag, n_off, False)

    def finish(qi, _):
        qrows = tile_rows(qi)
        zz = z_ref[qrows, :].astype(F32)
        o_ref[qrows, :] = (acc_scr[qi] * (zz * _sigmoid(zz))).astype(BF16)
        return 0

    lax.fori_loop(0, n_tiles, finish, 0)


def _sb_tile_table(n_tiles):
    tiles = [(i, i) for i in range(n_tiles)]
    tiles += [(i, j) for i in range(1, n_tiles) for j in range(i - 1, -1, -1)]
    return jnp.asarray([q for q, _ in tiles] + [k for _, k in tiles], dtype=jnp.int32)


def _stick_breaking(P, qg, kg, tri, B, S):
    n_pairs = SB_HEADS // 2
    pw = 2 * SB_HEAD_DIM
    tile = min(SB_TILE, S)
    n_tiles = S // tile
    head_of_lane = jnp.arange(pw) // SB_HEAD_DIM
    avg = (head_of_lane[:, None] == head_of_lane[None, :]).astype(BF16) * (1.0 / SB_HEAD_DIM)
    const2 = lambda b, p, tab: (0, 0)
    return pl.pallas_call(
        _sb_kernel,
        out_shape=jax.ShapeDtypeStruct((B * S, SB_W), BF16),
        grid_spec=pltpu.PrefetchScalarGridSpec(
            num_scalar_prefetch=1,
            grid=(B, n_pairs),
            in_specs=[
                pl.BlockSpec((S, pw), lambda b, p, tab: (b, P_SQ + p)),
                pl.BlockSpec((S, pw), lambda b, p, tab: (b, P_SK + p)),
                pl.BlockSpec((S, pw), lambda b, p, tab: (b, P_SV + p)),
                pl.BlockSpec((S, pw), lambda b, p, tab: (b, P_SZ + p)),
                pl.BlockSpec((1, pw), const2),
                pl.BlockSpec((1, pw), const2),
                pl.BlockSpec((tile, tile), const2),
                pl.BlockSpec((pw, pw), const2),
            ],
            out_specs=pl.BlockSpec((S, pw), lambda b, p, tab: (b, p)),
            scratch_shapes=[pltpu.VMEM((S, pw), BF16) for _ in range(5)] + [
                pltpu.VMEM((2, 2, tile, tile), BF16),
                pltpu.VMEM((2, 2, tile, tile), F32),
                pltpu.VMEM((2, 2, tile, SB_LANES), F32),
                pltpu.VMEM((2, tile, 2 * tile), BF16),
                pltpu.VMEM((n_tiles, 2, tile, SB_LANES), F32),
                pltpu.VMEM((n_tiles, tile, pw), F32),
            ]),
        compiler_params=pltpu.CompilerParams(
            dimension_semantics=("arbitrary", "arbitrary"), vmem_limit_bytes=VMEM_LIMIT),
        name="stickbreak",
    )(_sb_tile_table(n_tiles), P, P, P, P, qg, kg, tri, avg)


def _ssm_kernel(u_ref, are_ref, aim_ref, bre_ref, bim_ref, cre_ref, cimn_ref, d_ref, wg_ref, bg_ref,
                o_ref, bur_scr, bui_scr, hr_scr, hi_scr, cr_scr, ci_scr):
    T, B, _ = u_ref.shape
    rows_n = T * B
    half_k = SSM_W // 2
    half_n = SSM_STATES // 2

    @pl.when(pl.program_id(0) == 0)
    def _():
        cr_scr[...] = jnp.zeros_like(cr_scr)
        ci_scr[...] = jnp.zeros_like(ci_scr)

    u = u_ref[:, :, :SSM_W].reshape(rows_n, SSM_W)
    for c in range(2):
        uc = u[:, c * half_k:(c + 1) * half_k]
        bur_scr[:, c * half_n:(c + 1) * half_n] = _dot(uc, bre_ref[c])
        bui_scr[:, c * half_n:(c + 1) * half_n] = _dot(uc, bim_ref[c])

    for lc in range(SSM_STATES // SSM_LANE_CHUNK):
        ls = slice(lc * SSM_LANE_CHUNK, (lc + 1) * SSM_LANE_CHUNK)
        ar = jnp.broadcast_to(are_ref[:, ls], (B, SSM_LANE_CHUNK))
        ai = jnp.broadcast_to(aim_ref[:, ls], (B, SSM_LANE_CHUNK))

        def step(t, carry):
            hr, hi = carry
            rows = pl.ds(pl.multiple_of(t * B, B), B)
            nr = ar * hr - ai * hi + bur_scr[rows, ls]
            ni = ar * hi + ai * hr + bui_scr[rows, ls]
            hr_scr[rows, ls] = nr.astype(BF16)
            hi_scr[rows, ls] = ni.astype(BF16)
            return nr, ni

        hr, hi = lax.fori_loop(0, T, step, (cr_scr[:, ls], ci_scr[:, ls]), unroll=8)
        cr_scr[:, ls] = hr
        ci_scr[:, ls] = hi

    ys = []
    for c in range(2):
        ss = slice(c * half_n, (c + 1) * half_n)
        ys.append(_dot(hr_scr[:, ss], cre_ref[c]) + _dot(hi_scr[:, ss], cimn_ref[c]))
    y = jnp.concatenate(ys, axis=1) + d_ref[...] * u.astype(F32)
    y = 0.5 * y * (1.0 + jnp.tanh(math.sqrt(2.0 / math.pi) * (y + 0.044715 * (y * y * y))))
    y = y * _sigmoid(_dot(y.astype(BF16), wg_ref[...]) + bg_ref[...])
    cz = u_ref[:, :, SSM_W:].reshape(rows_n, SSM_W).astype(F32)
    o_ref[...] = (y * (cz * _sigmoid(cz))).astype(BF16).reshape(T, B, SSM_W)


def _ssm(U3, are, aim, bre, bim, cre, cimn, d, wg, bg, B, S):
    T = min(SSM_T, S)
    rows_n = T * B
    const2 = lambda t: (0, 0)
    const3 = lambda t: (0, 0, 0)
    return pl.pallas_call(
        _ssm_kernel,
        out_shape=jax.ShapeDtypeStruct((S, B, SSM_W), BF16),
        grid=(S // T,),
        in_specs=[
            pl.BlockSpec((T, B, COL_TILE), lambda t: (t, 0, 0)),
            pl.BlockSpec((1, SSM_STATES), const2),
            pl.BlockSpec((1, SSM_STATES), const2),
            pl.BlockSpec((2, SSM_W // 2, SSM_STATES // 2), const3),
            pl.BlockSpec((2, SSM_W // 2, SSM_STATES // 2), const3),
            pl.BlockSpec((2, SSM_STATES // 2, SSM_W // 2), const3),
            pl.BlockSpec((2, SSM_STATES // 2, SSM_W // 2), const3),
            pl.BlockSpec((1, SSM_W), const2),
            pl.BlockSpec((SSM_W, SSM_W), const2),
            pl.BlockSpec((1, SSM_W), const2),
        ],
        out_specs=pl.BlockSpec((T, B, SSM_W), lambda t: (t, 0, 0)),
        scratch_shapes=[
            pltpu.VMEM((rows_n, SSM_STATES), F32), pltpu.VMEM((rows_n, SSM_STATES), F32),
            pltpu.VMEM((rows_n, SSM_STATES), BF16), pltpu.VMEM((rows_n, SSM_STATES), BF16),
            pltpu.VMEM((B, SSM_STATES), F32), pltpu.VMEM((B, SSM_STATES), F32),
        ],
        compiler_params=pltpu.CompilerParams(
            dimension_semantics=("arbitrary",), vmem_limit_bytes=VMEM_LIMIT),
        name="ssm",
    )(U3, are, aim, bre, bim, cre, cimn, d, wg, bg)


def _merge_kernel(ya_ref, yb_ref, yc_ref, ga_ref, gb_ref, gc_ref, x_ref, pa_ref, pb_ref, pc_ref, wo_ref,
                  o_ref):
    tm = x_ref.shape[0]
    mc = min(512, tm)
    for r in range(0, tm, mc):
        rows = slice(r, r + mc)
        m = _sigmoid(ga_ref[rows, :].astype(F32)) * _dot(ya_ref[rows, :], pa_ref[...])
        m += _sigmoid(gb_ref[rows, :].astype(F32)) * _dot(yb_ref[rows, :], pb_ref[...])
        m += _sigmoid(gc_ref[rows, :].astype(F32)) * _dot(yc_ref[rows, :], pc_ref[...])
        o_ref[rows, :] = x_ref[rows, :] + _dot(m.astype(BF16), wo_ref[...])


def _merge(ya, yb, yc2d, P, x2d, pa, pb, pc, wo, B, S):
    tm = min(1024, S)
    nt = S // tm
    tok = lambda b, t: (b * nt + t, 0)
    const2 = lambda b, t: (0, 0)
    return pl.pallas_call(
        _merge_kernel,
        out_shape=jax.ShapeDtypeStruct((B * S, D_MODEL), F32),
        grid=(B, nt),
        in_specs=[
            pl.BlockSpec((tm, RET_V_W), tok),
            pl.BlockSpec((tm, SB_W), tok),
            pl.BlockSpec((tm, SSM_W), lambda b, t: (t, b)),
            pl.BlockSpec((tm, D_MODEL), lambda b, t: (b * nt + t, P_GA)),
            pl.BlockSpec((tm, D_MODEL), lambda b, t: (b * nt + t, P_GB)),
            pl.BlockSpec((tm, D_MODEL), lambda b, t: (b * nt + t, P_GC)),
            pl.BlockSpec((tm, D_MODEL), tok),
            pl.BlockSpec((RET_V_W, D_MODEL), const2),
            pl.BlockSpec((SB_W, D_MODEL), const2),
            pl.BlockSpec((SSM_W, D_MODEL), const2),
            pl.BlockSpec((D_MODEL, D_MODEL), const2),
        ],
        out_specs=pl.BlockSpec((tm, D_MODEL), tok),
        compiler_params=pltpu.CompilerParams(
            dimension_semantics=("arbitrary", "arbitrary"), vmem_limit_bytes=VMEM_LIMIT),
        name="merge",
    )(ya, yb, yc2d, P, P, P, x2d, pa, pb, pc, wo)


def _rope_tables(S):
    half = RET_QK_DIM // 2
    inv_freq = ROPE_BASE ** (-jnp.arange(half, dtype=F32) / half)
    ang = jnp.arange(S, dtype=F32)[:, None] * inv_freq[None, :]
    cos, sin = jnp.cos(ang), jnp.sin(ang)
    return jnp.concatenate([cos, cos], axis=1), jnp.concatenate([-sin, sin], axis=1)


def _decay_tables():
    log_gamma = jnp.log1p(-jnp.exp2(-5.0 - jnp.arange(RET_HEADS, dtype=F32)))
    idx = jnp.arange(RET_CHUNK, dtype=F32)
    rel = idx[:, None] - idx[None, :]
    dec = jnp.where(rel >= 0, jnp.exp(jnp.maximum(rel, 0.0)[None] * log_gamma[:, None, None]), 0.0)
    kdec = jnp.exp((RET_CHUNK - 1 - idx)[None, :] * log_gamma[:, None])
    qdec = jnp.exp((idx + 1)[None, :] * log_gamma[:, None])
    full = lambda a: jnp.broadcast_to(a[:, :, None], (RET_HEADS, RET_CHUNK, RET_QK_DIM))
    chunk_decay = tuple(float(math.exp(RET_CHUNK * math.log1p(-2.0 ** (-5.0 - h)))) for h in range(RET_HEADS))
    return dec, full(qdec), full(kdec), chunk_decay


def _ssm_tables(a_re, a_im, log_dt, b_re, b_im, c_re, c_im):
    dt = jnp.exp(log_dt)[:, None]
    mag = jnp.exp(dt * a_re)
    ab_re = mag * jnp.cos(dt * a_im)
    ab_im = mag * jnp.sin(dt * a_im)
    den = a_re * a_re + a_im * a_im
    nr = ab_re - 1.0
    coef_re = (nr * a_re + ab_im * a_im) / den
    coef_im = (ab_im * a_re - nr * a_im) / den
    bb_re = coef_re[..., None] * b_re - coef_im[..., None] * b_im
    bb_im = coef_re[..., None] * b_im + coef_im[..., None] * b_re
    gl = SSM_GROUPS // 2
    eye = jnp.eye(gl, dtype=F32)

    def b_blocks(bb):
        return jnp.einsum('cgpm,gh->cgmhp', bb.reshape(2, gl, SSM_STATE, SSM_GROUP), eye).reshape(
            2, gl * SSM_GROUP, gl * SSM_STATE).astype(BF16)

    def c_blocks(cc):
        return jnp.einsum('cgmp,gh->cgphm', cc.reshape(2, gl, SSM_GROUP, SSM_STATE), eye).reshape(
            2, gl * SSM_STATE, gl * SSM_GROUP).astype(BF16)

    return (ab_re.reshape(1, SSM_STATES), ab_im.reshape(1, SSM_STATES),
            b_blocks(bb_re), b_blocks(bb_im), c_blocks(c_re), c_blocks(-c_im))


def _suffix_sum_matrix(tile):
    k = jnp.arange(tile)
    return -(k[:, None] > k[None, :]).astype(BF16)


def kernel(x, norm_g, w_in, ret_q_norm, ret_k_norm, ret_out_norm, sb_q_norm, sb_k_norm, ssm_a_re, ssm_a_im,
           ssm_log_dt, ssm_b_re, ssm_b_im, ssm_c_re, ssm_c_im, ssm_d, ssm_w_glu, ssm_b_glu, proj_a, proj_b,
           proj_c, w_out):
    B, S, D = x.shape
    depth = w_in.shape[0]
    assert D == D_MODEL and B == 16 and S % 128 == 0
    cosf, sinf = _rope_tables(S)
    dec, qd, kd, chunk_decay = _decay_tables()
    tri = _suffix_sum_matrix(min(SB_TILE, S))
    x2d = x.reshape(B * S, D)
    for l in range(depth):
        P, U = _inproj(x2d, norm_g[l].reshape(1, D), w_in[l].astype(BF16), B, S)
        ya = _retention(P, cosf, sinf, dec, qd, kd, chunk_decay,
                        ret_q_norm[l], ret_k_norm[l] * (RET_QK_DIM ** -0.5),
                        ret_out_norm[l].reshape(1, -1), B, S)
        yb = _stick_breaking(P, jnp.tile(sb_q_norm[l], 2).reshape(1, -1),
                             jnp.tile(sb_k_norm[l], 2).reshape(1, -1), tri, B, S)
        are, aim, bre, bim, cre, cimn = _ssm_tables(ssm_a_re[l], ssm_a_im[l], ssm_log_dt[l], ssm_b_re[l],
                                                    ssm_b_im[l], ssm_c_re[l], ssm_c_im[l])
        yc = _ssm(U.reshape(S, B, COL_TILE), are, aim, bre, bim, cre, cimn,
                  ssm_d[l].reshape(1, -1), ssm_w_glu[l].astype(BF16), ssm_b_glu[l].reshape(1, -1), B, S)
        x2d = _merge(ya, yb, yc.reshape(S, B * SSM_W), P, x2d, proj_a[l].astype(BF16),
                     proj_b[l].astype(BF16), proj_c[l].astype(BF16), w_out[l].astype(BF16), B, S)
    return x2d.reshape(B, S, D)
```

```python
import functools
import math

import jax
import jax.numpy as jnp
from jax import lax
from jax.experimental import pallas as pl
from jax.experimental.pallas import tpu as pltpu

F32 = jnp.float32
BF16 = jnp.bfloat16

D_MODEL = 1024
RET_HEADS = 4
RET_QK_W = 512
RET_V_W = 1024
RET_QK_DIM = 128
RET_V_DIM = 256
RET_CHUNK = 128
ROPE_BASE = 10000.0
SB_HEADS = 8
SB_W = 512
SB_HEAD_DIM = 64
SB_TILE = 256
SB_LANES = 128
SB_MASKED_LOGIT = -1e30
SB_MAX_LOGIT = 126.0
SB_UNROLL = 4
LOG2E = math.log2(math.e)
SSM_W = 512
SSM_GROUP = 16
SSM_GROUPS = 32
SSM_STATE = 64
SSM_STATES = SSM_GROUPS * SSM_STATE
EPS = 1e-6
IN_W = 9216

COL_TILE = 1024
SSM_COL_TILE = 5
P_W = IN_W - COL_TILE
P_RQ, P_RK = 0, 1
P_RV, P_RZ = 1, 2
P_SQ, P_SK, P_SV, P_SZ = 24, 28, 32, 36
P_GA, P_GB, P_GC = 5, 6, 7

SSM_T = 32
SSM_LANE_CHUNK = 512
VMEM_LIMIT = 56 * 1024 * 1024


def _sigmoid(v):
    return 1.0 / (1.0 + jnp.exp(-v))


def _nt_dot(a, b):
    return lax.dot_general(a, b, (((1,), (1,)), ((), ())), preferred_element_type=F32)


def _dot(a, b):
    return jnp.dot(a, b, preferred_element_type=F32)


def _inproj_kernel(x_ref, g_ref, w_ref, p_ref, u_ref, h_scr):
    j = pl.program_id(2)
    tm = x_ref.shape[0]
    rc = min(256, tm)

    @pl.when(j == 0)
    def _():
        def body(i, _):
            rows = pl.ds(pl.multiple_of(i * rc, rc), rc)
            x = x_ref[rows, :]
            ms = jnp.mean(x * x, axis=-1, keepdims=True)
            h_scr[rows, :] = (x * lax.rsqrt(ms + EPS) * g_ref[...]).astype(BF16)
            return 0
        lax.fori_loop(0, tm // rc, body, 0)

    mc = tm

    @pl.when(j != SSM_COL_TILE)
    def _():
        for r in range(0, tm, mc):
            p_ref[r:r + mc, :] = _dot(h_scr[r:r + mc, :], w_ref[...]).astype(BF16)

    @pl.when(j == SSM_COL_TILE)
    def _():
        for r in range(0, tm, mc):
            u_ref[r:r + mc, :] = _dot(h_scr[r:r + mc, :], w_ref[...]).astype(BF16)


def _inproj(x2d, g, w, B, S):
    tm = min(1024, S)
    nt = S // tm
    n_col = IN_W // COL_TILE
    return pl.pallas_call(
        _inproj_kernel,
        out_shape=(jax.ShapeDtypeStruct((B * S, P_W), BF16),
                   jax.ShapeDtypeStruct((S, B * COL_TILE), BF16)),
        grid=(B, nt, n_col),
        in_specs=[
            pl.BlockSpec((tm, D_MODEL), lambda b, t, j: (b * nt + t, 0)),
            pl.BlockSpec((1, D_MODEL), lambda b, t, j: (0, 0)),
            pl.BlockSpec((D_MODEL, COL_TILE), lambda b, t, j: (0, j)),
        ],
        out_specs=(
            pl.BlockSpec((tm, COL_TILE),
                         lambda b, t, j: (b * nt + t, j - (j >= SSM_COL_TILE).astype(jnp.int32))),
            pl.BlockSpec((tm, COL_TILE), lambda b, t, j: (t, b)),
        ),
        scratch_shapes=[pltpu.VMEM((tm, D_MODEL), BF16)],
        compiler_params=pltpu.CompilerParams(
            dimension_semantics=("arbitrary", "arbitrary", "arbitrary"),
            vmem_limit_bytes=VMEM_LIMIT),
        name="inproj",
    )(x2d, g, w)


def _ret_kernel(chunk_decay, q_ref, k_ref, v_ref, z_ref, cos_ref, sin_ref, dec_ref, qd_ref, kd_ref,
                qg_ref, kg_ref, og_ref, perm_ref, avg_ref, eye_ref, o_ref,
                qr_scr, qdec_scr, kr_scr, kdt_scr, st_scr, stb_scr, sd_scr, cr_scr):
    S = q_ref.shape[0]
    n_chunks = S // RET_CHUNK
    heads = range(RET_HEADS)
    qk = [slice(h * RET_QK_DIM, (h + 1) * RET_QK_DIM) for h in heads]
    vv = [slice(h * RET_V_DIM, (h + 1) * RET_V_DIM) for h in heads]

    def chunk_rows(c):
        return pl.ds(pl.multiple_of(c * RET_CHUNK, RET_CHUNK), RET_CHUNK)

    def prep(c, _):
        rows = chunk_rows(c)
        cosf = cos_ref[rows, :]
        sinf = sin_ref[rows, :]
        gains = ((qg_ref[0:1, :] * cosf, qg_ref[1:2, :] * sinf), (kg_ref[0:1, :] * cosf, kg_ref[1:2, :] * sinf))
        raw = [[ref[rows, qk[h]] for h in heads] for ref in (q_ref, k_ref)]
        rolled = [[_dot(t, perm_ref[...]) for t in ts] for ts in raw]
        f32 = [[t.astype(F32) for t in ts] for ts in raw]
        mean_sq = [[_dot((t * t).astype(BF16), avg_ref[...]) for t in ts] for ts in f32]
        rot = [[(f32[i][h] * gains[i][0] + rolled[i][h] * gains[i][1]) * lax.rsqrt(mean_sq[i][h] + EPS)
                for h in heads] for i in range(2)]
        kdec = [(rot[1][h] * kd_ref[h]).astype(BF16) for h in heads]
        kdec_t = [_nt_dot(eye_ref[...], t) for t in kdec]
        for h in heads:
            qr_scr[rows, qk[h]] = rot[0][h].astype(BF16)
            qdec_scr[rows, qk[h]] = (rot[0][h] * qd_ref[h]).astype(BF16)
            kr_scr[rows, qk[h]] = rot[1][h].astype(BF16)
            kdt_scr[c, h] = kdec_t[h].astype(BF16)
        return 0

    lax.fori_loop(0, n_chunks, prep, 0)

    st_scr[...] = jnp.zeros_like(st_scr)
    stb_scr[...] = jnp.zeros_like(stb_scr)

    def state_matmuls(c):
        rows = chunk_rows(c)
        scores = [_nt_dot(qr_scr[rows, qk[h]], kr_scr[rows, qk[h]]) for h in heads]
        cross = [_dot(qdec_scr[rows, qk[h]], stb_scr[h]) for h in heads]
        kv = [_dot(kdt_scr[c, h], v_ref[rows, vv[h]]) for h in heads]
        return scores, cross, kv

    def state_update(slot, scores, cross, kv):
        for h in heads:
            sd_scr[slot, h] = (scores[h] * dec_ref[h]).astype(BF16)
            cr_scr[slot, h] = cross[h]
            st = st_scr[h] * chunk_decay[h] + kv[h]
            st_scr[h] = st
            stb_scr[h] = st.astype(BF16)

    def inner_matmuls(c, slot):
        rows = chunk_rows(c)
        return [_dot(sd_scr[slot, h], v_ref[rows, vv[h]]) for h in heads]

    def write_out(c, slot, inner):
        rows = chunk_rows(c)
        for h in heads:
            o = inner[h] + cr_scr[slot, h]
            on = o * lax.rsqrt(jnp.mean(o * o, axis=-1, keepdims=True) + EPS) * og_ref[:, vv[h]]
            zz = z_ref[rows, vv[h]].astype(F32)
            o_ref[rows, vv[h]] = (on * (zz * _sigmoid(zz))).astype(BF16)

    state_update(0, *state_matmuls(0))

    def step(c, _):
        slot = c % 2
        sm = state_matmuls(c)
        inner = inner_matmuls(c - 1, 1 - slot)
        state_update(slot, *sm)
        write_out(c - 1, 1 - slot, inner)
        return 0

    lax.fori_loop(1, n_chunks, step, 0)
    last = n_chunks - 1
    write_out(last, last % 2, inner_matmuls(last, last % 2))


def _retention(P, cosf, sinf, dec, qd, kd, chunk_decay, qg, kg, og, B, S):
    half = RET_QK_DIM // 2
    lane = jnp.arange(RET_QK_DIM)
    perm = (lane[:, None] == (lane[None, :] + half) % RET_QK_DIM).astype(BF16)
    avg = jnp.full((RET_QK_DIM, RET_QK_DIM), 1.0 / RET_QK_DIM, BF16)
    eye = jnp.eye(RET_QK_DIM, dtype=BF16)
    with_rolled = lambda g: jnp.stack([g, jnp.roll(g, half)])
    const2 = lambda b: (0, 0)
    const3 = lambda b: (0, 0, 0)
    return pl.pallas_call(
        functools.partial(_ret_kernel, chunk_decay),
        out_shape=jax.ShapeDtypeStruct((B * S, RET_V_W), BF16),
        grid=(B,),
        in_specs=[
            pl.BlockSpec((S, RET_QK_W), lambda b: (b, P_RQ)),
            pl.BlockSpec((S, RET_QK_W), lambda b: (b, P_RK)),
            pl.BlockSpec((S, RET_V_W), lambda b: (b, P_RV)),
            pl.BlockSpec((S, RET_V_W), lambda b: (b, P_RZ)),
            pl.BlockSpec((S, RET_QK_DIM), const2),
            pl.BlockSpec((S, RET_QK_DIM), const2),
            pl.BlockSpec((RET_HEADS, RET_CHUNK, RET_CHUNK), const3),
            pl.BlockSpec((RET_HEADS, RET_CHUNK, RET_QK_DIM), const3),
            pl.BlockSpec((RET_HEADS, RET_CHUNK, RET_QK_DIM), const3),
            pl.BlockSpec((2, RET_QK_DIM), const2),
            pl.BlockSpec((2, RET_QK_DIM), const2),
            pl.BlockSpec((1, RET_V_W), const2),
            pl.BlockSpec((RET_QK_DIM, RET_QK_DIM), const2),
            pl.BlockSpec((RET_QK_DIM, RET_QK_DIM), const2),
            pl.BlockSpec((RET_QK_DIM, RET_QK_DIM), const2),
        ],
        out_specs=pl.BlockSpec((S, RET_V_W), lambda b: (b, 0)),
        scratch_shapes=[
            pltpu.VMEM((S, RET_QK_W), BF16),
            pltpu.VMEM((S, RET_QK_W), BF16),
            pltpu.VMEM((S, RET_QK_W), BF16),
            pltpu.VMEM((S // RET_CHUNK, RET_HEADS, RET_QK_DIM, RET_CHUNK), BF16),
            pltpu.VMEM((RET_HEADS, RET_QK_DIM, RET_V_DIM), F32),
            pltpu.VMEM((RET_HEADS, RET_QK_DIM, RET_V_DIM), BF16),
            pltpu.VMEM((2, RET_HEADS, RET_CHUNK, RET_CHUNK), BF16),
            pltpu.VMEM((2, RET_HEADS, RET_CHUNK, RET_V_DIM), F32),
        ],
        compiler_params=pltpu.CompilerParams(
            dimension_semantics=("arbitrary",), vmem_limit_bytes=VMEM_LIMIT),
        name="retention",
    )(P, P, P, P, cosf, sinf, dec, qd, kd, with_rolled(qg), with_rolled(kg), og, perm, avg, eye)


def _sb_kernel(tab_ref, q_ref, k_ref, v_ref, z_ref, qg_ref, kg_ref, tri_ref, avg_ref, o_ref,
               qlo_scr, qhi_scr, kn_scr, vlo_scr, vhi_scr, nlk_scr, s_scr, rs_scr, w_scr, carry_scr, acc_scr):
    S = q_ref.shape[0]
    nc = min(256, S)
    lane_n = lax.broadcasted_iota(jnp.int32, (nc, 2 * SB_HEAD_DIM), 1)
    lo_n = lane_n < SB_HEAD_DIM

    def head_rms(ref, rows, g_ref):
        x = ref[rows, :].astype(F32)
        ms = _dot((x * x).astype(BF16), avg_ref[...])
        return x * lax.rsqrt(ms + EPS) * g_ref[...]

    def norm_body(i, _):
        rows = pl.ds(pl.multiple_of(i * nc, nc), nc)
        qn = head_rms(q_ref, rows, qg_ref) * (SB_HEAD_DIM ** -0.5 * LOG2E)
        qlo_scr[rows, :] = jnp.where(lo_n, qn, 0.0).astype(BF16)
        qhi_scr[rows, :] = jnp.where(lo_n, 0.0, qn).astype(BF16)
        kn_scr[rows, :] = head_rms(k_ref, rows, kg_ref).astype(BF16)
        v = v_ref[rows, :].astype(F32)
        vlo_scr[rows, :] = jnp.where(lo_n, v, 0.0).astype(BF16)
        vhi_scr[rows, :] = jnp.where(lo_n, 0.0, v).astype(BF16)
        return 0

    lax.fori_loop(0, S // nc, norm_body, 0)

    tile = nlk_scr.shape[2]
    n_tiles = S // tile
    n_sub = tile // SB_LANES
    n_diag = n_tiles
    n_off = n_tiles * (n_tiles - 1) // 2
    n_tab = n_diag + n_off
    qi_idx = lax.broadcasted_iota(jnp.int32, (tile, tile), 0)
    ki_idx = lax.broadcasted_iota(jnp.int32, (tile, tile), 1)
    causal = ki_idx < qi_idx

    def tile_rows(idx):
        return pl.ds(pl.multiple_of(idx * tile, tile), tile)

    def stage_scores(n, slot, masked):
        k2 = kn_scr[tile_rows(tab_ref[n_tab + n]), :]
        qrows = tile_rows(tab_ref[n])
        for hh, q_scr in enumerate((qlo_scr, qhi_scr)):
            z = jnp.minimum(_nt_dot(q_scr[qrows, :], k2), SB_MAX_LOGIT)
            nlk = jnp.log(1.0 + jnp.exp2(z)) * LOG2E
            s = z - nlk
            if masked:
                nlk = jnp.where(causal, nlk, 0.0)
                s = jnp.where(causal, s, SB_MASKED_LOGIT)
            nlk_scr[slot, hh] = nlk.astype(BF16)
            s_scr[slot, hh] = s
            rs_scr[slot, hh] = jnp.broadcast_to(jnp.sum(nlk, axis=-1, keepdims=True), (tile, SB_LANES))

    def stage_weights(n, slot, first):
        qi = tab_ref[n]
        for hh in range(2):
            carry = jnp.zeros((tile, SB_LANES), F32) if first else carry_scr[qi, hh]
            after = _dot(nlk_scr[slot, hh], tri_ref[...]) + jnp.concatenate([carry] * n_sub, axis=1)
            w = jnp.exp2(s_scr[slot, hh] + after)
            w_scr[slot, :, hh * tile:(hh + 1) * tile] = w.astype(BF16)
            carry_scr[qi, hh] = carry - rs_scr[slot, hh]

    def stage_values(n, slot, first):
        krows = tile_rows(tab_ref[n_tab + n])
        qi = tab_ref[n]
        vcat = jnp.concatenate([vlo_scr[krows, :], vhi_scr[krows, :]], axis=0)
        contrib = _dot(w_scr[slot], vcat)
        if first:
            acc_scr[qi] = contrib
        else:
            acc_scr[qi] += contrib

    def run_pipelined(n0, count, diag):
        def step(it, parity, a, b, c):
            if a:
                stage_scores(n0 + it, parity, diag)
            if b:
                stage_weights(n0 + it - 1, 1 - parity, diag)
            if c:
                stage_values(n0 + it - 2, parity, diag)

        def static_step(it):
            step(it, it % 2, 0 <= it < count, 0 <= it - 1 < count, 0 <= it - 2 < count)

        steady = list(range(2, count))
        for it in (0, 1):
            static_step(it)
        while len(steady) % SB_UNROLL:
            static_step(steady.pop(0))
        if steady:
            first = steady[0]

            def group(p, _):
                for u in range(SB_UNROLL):
                    step(first + SB_UNROLL * p + u, (first + u) % 2, True, True, True)
                return 0

            lax.fori_loop(0, len(steady) // SB_UNROLL, group, 0)
        for it in sorted({count, count + 1} - {0, 1}):
            static_step(it)

    run_pipelined(0, n_diag, True)
    run_pipelined(n_diag, n_off, False)

    def finish(qi, _):
        qrows = tile_rows(qi)
        zz = z_ref[qrows, :].astype(F32)
        o_ref[qrows, :] = (acc_scr[qi] * (zz * _sigmoid(zz))).astype(BF16)
        return 0

    lax.fori_loop(0, n_tiles, finish, 0)


def _sb_tile_table(n_tiles):
    tiles = [(i, i) for i in range(n_tiles)]
    tiles += [(i, j) for i in range(1, n_tiles) for j in range(i - 1, -1, -1)]
    return jnp.asarray([q for q, _ in tiles] + [k for _, k in tiles], dtype=jnp.int32)


def _stick_breaking(P, qg, kg, tri, B, S):
    n_pairs = SB_HEADS // 2
    pw = 2 * SB_HEAD_DIM
    tile = min(SB_TILE, S)
    n_tiles = S // tile
    head_of_lane = jnp.arange(pw) // SB_HEAD_DIM
    avg = (head_of_lane[:, None] == head_of_lane[None, :]).astype(BF16) * (1.0 / SB_HEAD_DIM)
    const2 = lambda b, p, tab: (0, 0)
    return pl.pallas_call(
        _sb_kernel,
        out_shape=jax.ShapeDtypeStruct((B * S, SB_W), BF16),
        grid_spec=pltpu.PrefetchScalarGridSpec(
            num_scalar_prefetch=1,
            grid=(B, n_pairs),
            in_specs=[
                pl.BlockSpec((S, pw), lambda b, p, tab: (b, P_SQ + p)),
                pl.BlockSpec((S, pw), lambda b, p, tab: (b, P_SK + p)),
                pl.BlockSpec((S, pw), lambda b, p, tab: (b, P_SV + p)),
                pl.BlockSpec((S, pw), lambda b, p, tab: (b, P_SZ + p)),
                pl.BlockSpec((1, pw), const2),
                pl.BlockSpec((1, pw), const2),
                pl.BlockSpec((tile, tile), const2),
                pl.BlockSpec((pw, pw), const2),
            ],
            out_specs=pl.BlockSpec((S, pw), lambda b, p, tab: (b, p)),
            scratch_shapes=[pltpu.VMEM((S, pw), BF16) for _ in range(5)] + [
                pltpu.VMEM((2, 2, tile, tile), BF16),
                pltpu.VMEM((2, 2, tile, tile), F32),
                pltpu.VMEM((2, 2, tile, SB_LANES), F32),
                pltpu.VMEM((2, tile, 2 * tile), BF16),
                pltpu.VMEM((n_tiles, 2, tile, SB_LANES), F32),
                pltpu.VMEM((n_tiles, tile, pw), F32),
            ]),
        compiler_params=pltpu.CompilerParams(
            dimension_semantics=("arbitrary", "arbitrary"), vmem_limit_bytes=VMEM_LIMIT),
        name="stickbreak",
    )(_sb_tile_table(n_tiles), P, P, P, P, qg, kg, tri, avg)


def _ssm_kernel(u_ref, uo_ref, are_ref, aim_ref, bre_ref, bim_ref, cre_ref, cimn_ref, d_ref, wg_ref, bg_ref,
                o_ref, bur0, bur1, bui0, bui1, hr0, hr1, hi0, hi1, cr_scr, ci_scr):
    g = pl.program_id(0)

    @pl.when(g == 0)
    def _():
        for ref in (cr_scr, ci_scr, bur1, bui1, hr0, hi0):
            ref[...] = jnp.zeros_like(ref)

    even = (bur0, bui0, hr0, hi0)
    odd = (bur1, bui1, hr1, hi1)
    refs = (u_ref, uo_ref, are_ref, aim_ref, bre_ref, bim_ref, cre_ref, cimn_ref, d_ref, wg_ref, bg_ref,
            o_ref, cr_scr, ci_scr)

    @pl.when(g % 2 == 0)
    def _():
        _ssm_step(refs, even, odd)

    @pl.when(g % 2 == 1)
    def _():
        _ssm_step(refs, odd, even)


def _ssm_step(refs, new, old):
    (u_ref, uo_ref, are_ref, aim_ref, bre_ref, bim_ref, cre_ref, cimn_ref, d_ref, wg_ref, bg_ref,
     o_ref, cr_scr, ci_scr) = refs
    bur_new, bui_new, hr_new, hi_new = new
    bur_old, bui_old, hr_old, hi_old = old
    T, B, _ = u_ref.shape
    rows_n = T * B
    half_k = SSM_W // 2
    half_n = SSM_STATES // 2

    def scan(lc):
        ls = slice(lc * SSM_LANE_CHUNK, (lc + 1) * SSM_LANE_CHUNK)
        ar = jnp.broadcast_to(are_ref[:, ls], (B, SSM_LANE_CHUNK))
        ai = jnp.broadcast_to(aim_ref[:, ls], (B, SSM_LANE_CHUNK))
        hr = cr_scr[:, ls]
        hi = ci_scr[:, ls]
        for t in range(T):
            rows = slice(t * B, (t + 1) * B)
            hr, hi = (ar * hr - ai * hi + bur_old[rows, ls],
                      ar * hi + ai * hr + bui_old[rows, ls])
            hr_old[rows, ls] = hr.astype(BF16)
            hi_old[rows, ls] = hi.astype(BF16)
        cr_scr[:, ls] = hr
        ci_scr[:, ls] = hi

    lane_chunks = list(range(SSM_STATES // SSM_LANE_CHUNK))

    ys = []
    for c in range(2):
        ss = slice(c * half_n, (c + 1) * half_n)
        ys.append(_dot(hr_new[:, ss], cre_ref[c]) + _dot(hi_new[:, ss], cimn_ref[c]))
        scan(lane_chunks.pop(0))

    u = u_ref[:, :, :SSM_W].reshape(rows_n, SSM_W)
    for c in range(2):
        uc = u[:, c * half_k:(c + 1) * half_k]
        bur_new[:, c * half_n:(c + 1) * half_n] = _dot(uc, bre_ref[c])
        bui_new[:, c * half_n:(c + 1) * half_n] = _dot(uc, bim_ref[c])
        scan(lane_chunks.pop(0))
    assert not lane_chunks

    uo = uo_ref[:, :, :SSM_W].reshape(rows_n, SSM_W).astype(F32)
    y = jnp.concatenate(ys, axis=1) + d_ref[...] * uo
    y = 0.5 * y * (1.0 + jnp.tanh(math.sqrt(2.0 / math.pi) * (y + 0.044715 * (y * y * y))))
    y = y * _sigmoid(_dot(y.astype(BF16), wg_ref[...]) + bg_ref[...])
    cz = uo_ref[:, :, SSM_W:].reshape(rows_n, SSM_W).astype(F32)
    o_ref[...] = (y * (cz * _sigmoid(cz))).astype(BF16).reshape(T, B, SSM_W)


def _ssm(U3, are, aim, bre, bim, cre, cimn, d, wg, bg, B, S):
    T = min(SSM_T, S)
    rows_n = T * B
    n = S // T
    const2 = lambda g: (0, 0)
    const3 = lambda g: (0, 0, 0)
    cur = lambda g: (jnp.minimum(g, n - 1), 0, 0)
    done = lambda g: (jnp.clip(g - 2, 0, n - 1), 0, 0)
    return pl.pallas_call(
        _ssm_kernel,
        out_shape=jax.ShapeDtypeStruct((S, B, SSM_W), BF16),
        grid=(n + 2,),
        in_specs=[
            pl.BlockSpec((T, B, COL_TILE), cur),
            pl.BlockSpec((T, B, COL_TILE), done),
            pl.BlockSpec((1, SSM_STATES), const2),
            pl.BlockSpec((1, SSM_STATES), const2),
            pl.BlockSpec((2, SSM_W // 2, SSM_STATES // 2), const3),
            pl.BlockSpec((2, SSM_W // 2, SSM_STATES // 2), const3),
            pl.BlockSpec((2, SSM_STATES // 2, SSM_W // 2), const3),
            pl.BlockSpec((2, SSM_STATES // 2, SSM_W // 2), const3),
            pl.BlockSpec((1, SSM_W), const2),
            pl.BlockSpec((SSM_W, SSM_W), const2),
            pl.BlockSpec((1, SSM_W), const2),
        ],
        out_specs=pl.BlockSpec((T, B, SSM_W), done),
        scratch_shapes=[
            *[pltpu.VMEM((rows_n, SSM_STATES), F32)] * 4,
            *[pltpu.VMEM((rows_n, SSM_STATES), BF16)] * 4,
            *[pltpu.VMEM((B, SSM_STATES), F32)] * 2,
        ],
        compiler_params=pltpu.CompilerParams(
            dimension_semantics=("arbitrary",), vmem_limit_bytes=VMEM_LIMIT),
        name="ssm",
    )(U3, U3, are, aim, bre, bim, cre, cimn, d, wg, bg)


def _merge_kernel(ya_ref, yb_ref, yc_ref, ga_ref, gb_ref, gc_ref, x_ref, pa_ref, pb_ref, pc_ref, wo_ref,
                  o_ref):
    tm = x_ref.shape[0]
    mc = min(512, tm)
    for r in range(0, tm, mc):
        rows = slice(r, r + mc)
        m = _sigmoid(ga_ref[rows, :].astype(F32)) * _dot(ya_ref[rows, :], pa_ref[...])
        m += _sigmoid(gb_ref[rows, :].astype(F32)) * _dot(yb_ref[rows, :], pb_ref[...])
        m += _sigmoid(gc_ref[rows, :].astype(F32)) * _dot(yc_ref[rows, :], pc_ref[...])
        o_ref[rows, :] = x_ref[rows, :] + _dot(m.astype(BF16), wo_ref[...])


def _merge(ya, yb, yc2d, P, x2d, pa, pb, pc, wo, B, S):
    tm = min(1024, S)
    nt = S // tm
    tok = lambda b, t: (b * nt + t, 0)
    const2 = lambda b, t: (0, 0)
    return pl.pallas_call(
        _merge_kernel,
        out_shape=jax.ShapeDtypeStruct((B * S, D_MODEL), F32),
        grid=(B, nt),
        in_specs=[
            pl.BlockSpec((tm, RET_V_W), tok),
            pl.BlockSpec((tm, SB_W), tok),
            pl.BlockSpec((tm, SSM_W), lambda b, t: (t, b)),
            pl.BlockSpec((tm, D_MODEL), lambda b, t: (b * nt + t, P_GA)),
            pl.BlockSpec((tm, D_MODEL), lambda b, t: (b * nt + t, P_GB)),
            pl.BlockSpec((tm, D_MODEL), lambda b, t: (b * nt + t, P_GC)),
            pl.BlockSpec((tm, D_MODEL), tok),
            pl.BlockSpec((RET_V_W, D_MODEL), const2),
            pl.BlockSpec((SB_W, D_MODEL), const2),
            pl.BlockSpec((SSM_W, D_MODEL), const2),
            pl.BlockSpec((D_MODEL, D_MODEL), const2),
        ],
        out_specs=pl.BlockSpec((tm, D_MODEL), tok),
        compiler_params=pltpu.CompilerParams(
            dimension_semantics=("arbitrary", "arbitrary"), vmem_limit_bytes=VMEM_LIMIT),
        name="merge",
    )(ya, yb, yc2d, P, P, P, x2d, pa, pb, pc, wo)


def _rope_tables(S):
    half = RET_QK_DIM // 2
    inv_freq = ROPE_BASE ** (-jnp.arange(half, dtype=F32) / half)
    ang = jnp.arange(S, dtype=F32)[:, None] * inv_freq[None, :]
    cos, sin = jnp.cos(ang), jnp.sin(ang)
    return jnp.concatenate([cos, cos], axis=1), jnp.concatenate([-sin, sin], axis=1)


def _decay_tables():
    log_gamma = jnp.log1p(-jnp.exp2(-5.0 - jnp.arange(RET_HEADS, dtype=F32)))
    idx = jnp.arange(RET_CHUNK, dtype=F32)
    rel = idx[:, None] - idx[None, :]
    dec = jnp.where(rel >= 0, jnp.exp(jnp.maximum(rel, 0.0)[None] * log_gamma[:, None, None]), 0.0)
    kdec = jnp.exp((RET_CHUNK - 1 - idx)[None, :] * log_gamma[:, None])
    qdec = jnp.exp((idx + 1)[None, :] * log_gamma[:, None])
    full = lambda a: jnp.broadcast_to(a[:, :, None], (RET_HEADS, RET_CHUNK, RET_QK_DIM))
    chunk_decay = tuple(float(math.exp(RET_CHUNK * math.log1p(-2.0 ** (-5.0 - h)))) for h in range(RET_HEADS))
    return dec, full(qdec), full(kdec), chunk_decay


def _ssm_tables(a_re, a_im, log_dt, b_re, b_im, c_re, c_im):
    dt = jnp.exp(log_dt)[:, None]
    mag = jnp.exp(dt * a_re)
    ab_re = mag * jnp.cos(dt * a_im)
    ab_im = mag * jnp.sin(dt * a_im)
    den = a_re * a_re + a_im * a_im
    nr = ab_re - 1.0
    coef_re = (nr * a_re + ab_im * a_im) / den
    coef_im = (ab_im * a_re - nr * a_im) / den
    bb_re = coef_re[..., None] * b_re - coef_im[..., None] * b_im
    bb_im = coef_re[..., None] * b_im + coef_im[..., None] * b_re
    gl = SSM_GROUPS // 2
    eye = jnp.eye(gl, dtype=F32)

    def b_blocks(bb):
        return jnp.einsum('cgpm,gh->cgmhp', bb.reshape(2, gl, SSM_STATE, SSM_GROUP), eye).reshape(
            2, gl * SSM_GROUP, gl * SSM_STATE).astype(BF16)

    def c_blocks(cc):
        return jnp.einsum('cgmp,gh->cgphm', cc.reshape(2, gl, SSM_GROUP, SSM_STATE), eye).reshape(
            2, gl * SSM_STATE, gl * SSM_GROUP).astype(BF16)

    return (ab_re.reshape(1, SSM_STATES), ab_im.reshape(1, SSM_STATES),
            b_blocks(bb_re), b_blocks(bb_im), c_blocks(c_re), c_blocks(-c_im))


def _suffix_sum_matrix(tile):
    k = jnp.arange(tile)
    return -(k[:, None] > k[None, :]).astype(BF16)


def kernel(x, norm_g, w_in, ret_q_norm, ret_k_norm, ret_out_norm, sb_q_norm, sb_k_norm, ssm_a_re, ssm_a_im,
           ssm_log_dt, ssm_b_re, ssm_b_im, ssm_c_re, ssm_c_im, ssm_d, ssm_w_glu, ssm_b_glu, proj_a, proj_b,
           proj_c, w_out):
    B, S, D = x.shape
    depth = w_in.shape[0]
    assert D == D_MODEL and B == 16 and S % 128 == 0
    cosf, sinf = _rope_tables(S)
    dec, qd, kd, chunk_decay = _decay_tables()
    tri = _suffix_sum_matrix(min(SB_TILE, S))
    x2d = x.reshape(B * S, D)
    for l in range(depth):
        P, U = _inproj(x2d, norm_g[l].reshape(1, D), w_in[l].astype(BF16), B, S)
        ya = _retention(P, cosf, sinf, dec, qd, kd, chunk_decay,
                        ret_q_norm[l], ret_k_norm[l] * (RET_QK_DIM ** -0.5),
                        ret_out_norm[l].reshape(1, -1), B, S)
        yb = _stick_breaking(P, jnp.tile(sb_q_norm[l], 2).reshape(1, -1),
                             jnp.tile(sb_k_norm[l], 2).reshape(1, -1), tri, B, S)
        are, aim, bre, bim, cre, cimn = _ssm_tables(ssm_a_re[l], ssm_a_im[l], ssm_log_dt[l], ssm_b_re[l],
                                                    ssm_b_im[l], ssm_c_re[l], ssm_c_im[l])
        yc = _ssm(U.reshape(S, B, COL_TILE), are, aim, bre, bim, cre, cimn,
                  ssm_d[l].reshape(1, -1), ssm_w_glu[l].astype(BF16), ssm_b_glu[l].reshape(1, -1), B, S)
        x2d = _merge(ya, yb, yc.reshape(S, B * SSM_W), P, x2d, proj_a[l].astype(BF16),
                     proj_b[l].astype(BF16), proj_c[l].astype(BF16), w_out[l].astype(BF16), B, S)
    return x2d.reshape(B, S, D)
```

```python
import functools
import math

import jax
import jax.numpy as jnp
from jax import lax
from jax.experimental import pallas as pl
from jax.experimental.pallas import tpu as pltpu

F32 = jnp.float32
BF16 = jnp.bfloat16

D_MODEL = 1024
RET_HEADS = 4
RET_QK_W = 512
RET_V_W = 1024
RET_QK_DIM = 128
RET_V_DIM = 256
RET_CHUNK = 128
RET_PREP_CHUNKS = 2
ROPE_BASE = 10000.0
SB_HEADS = 8
SB_W = 512
SB_HEAD_DIM = 64
SB_TILE = 256
SB_LANES = 128
SB_MASKED_LOGIT = -1e30
SB_MAX_LOGIT = 126.0
SB_UNROLL = 4
LOG2E = math.log2(math.e)
SSM_W = 512
SSM_GROUP = 16
SSM_GROUPS = 32
SSM_STATE = 64
SSM_STATES = SSM_GROUPS * SSM_STATE
EPS = 1e-6
IN_W = 9216

COL_TILE = 1024
SSM_COL_TILE = 5
P_W = IN_W - COL_TILE
P_RQ, P_RK = 0, 1
P_RV, P_RZ = 1, 2
P_SQ, P_SK, P_SV, P_SZ = 24, 28, 32, 36
P_GA, P_GB, P_GC = 5, 6, 7

SSM_T = 32
SSM_LANE_CHUNK = 512
VMEM_LIMIT = 56 * 1024 * 1024


def _sigmoid(v):
    return 1.0 / (1.0 + jnp.exp(-v))


def _nt_dot(a, b):
    return lax.dot_general(a, b, (((1,), (1,)), ((), ())), preferred_element_type=F32)


def _dot(a, b):
    return jnp.dot(a, b, preferred_element_type=F32)


def _inproj_kernel(x_ref, g_ref, w_ref, p_ref, u_ref, h_scr):
    j = pl.program_id(2)
    tm = x_ref.shape[0]
    rc = min(256, tm)

    @pl.when(j == 0)
    def _():
        def body(i, _):
            rows = pl.ds(pl.multiple_of(i * rc, rc), rc)
            x = x_ref[rows, :]
            ms = jnp.mean(x * x, axis=-1, keepdims=True)
            h_scr[rows, :] = (x * lax.rsqrt(ms + EPS) * g_ref[...]).astype(BF16)
            return 0
        lax.fori_loop(0, tm // rc, body, 0)

    mc = tm

    @pl.when(j != SSM_COL_TILE)
    def _():
        for r in range(0, tm, mc):
            p_ref[r:r + mc, :] = _dot(h_scr[r:r + mc, :], w_ref[...]).astype(BF16)

    @pl.when(j == SSM_COL_TILE)
    def _():
        for r in range(0, tm, mc):
            u_ref[r:r + mc, :] = _dot(h_scr[r:r + mc, :], w_ref[...]).astype(BF16)


def _inproj(x2d, g, w_all, layer, B, S):
    tm = min(1024, S)
    nt = S // tm
    n_col = IN_W // COL_TILE
    return pl.pallas_call(
        _inproj_kernel,
        out_shape=(jax.ShapeDtypeStruct((B * S, P_W), BF16),
                   jax.ShapeDtypeStruct((S, B * COL_TILE), BF16)),
        grid=(B, nt, n_col),
        in_specs=[
            pl.BlockSpec((tm, D_MODEL), lambda b, t, j: (b * nt + t, 0)),
            pl.BlockSpec((1, D_MODEL), lambda b, t, j: (0, 0)),
            pl.BlockSpec((None, D_MODEL, COL_TILE), lambda b, t, j: (layer, 0, j)),
        ],
        out_specs=(
            pl.BlockSpec((tm, COL_TILE),
                         lambda b, t, j: (b * nt + t, j - (j >= SSM_COL_TILE).astype(jnp.int32))),
            pl.BlockSpec((tm, COL_TILE), lambda b, t, j: (t, b)),
        ),
        scratch_shapes=[pltpu.VMEM((tm, D_MODEL), BF16)],
        compiler_params=pltpu.CompilerParams(
            dimension_semantics=("arbitrary", "arbitrary", "arbitrary"),
            vmem_limit_bytes=VMEM_LIMIT),
        name="inproj",
    )(x2d, g, w_all)


def _ret_kernel(chunk_decay, q_ref, k_ref, v_ref, z_ref, cos_ref, sin_ref, dec_ref, qd_ref, kd_ref,
                qg_ref, kg_ref, og_ref, perm_ref, avg_ref, eye_ref, o_ref,
                qr_scr, qdec_scr, kr_scr, kdt_scr, st_scr, stb_scr, sd_scr, cr_scr):
    S = q_ref.shape[0]
    n_chunks = S // RET_CHUNK
    heads = range(RET_HEADS)
    qk = [slice(h * RET_QK_DIM, (h + 1) * RET_QK_DIM) for h in heads]
    vv = [slice(h * RET_V_DIM, (h + 1) * RET_V_DIM) for h in heads]

    def chunk_rows(c):
        return pl.ds(pl.multiple_of(c * RET_CHUNK, RET_CHUNK), RET_CHUNK)

    def prep(i, _):
        cs = [i * RET_PREP_CHUNKS + u for u in range(min(RET_PREP_CHUNKS, n_chunks))]
        rows = [chunk_rows(c) for c in cs]
        raw = [[[ref[r, qk[h]] for h in heads] for ref in (q_ref, k_ref)] for r in rows]
        rolled = [[[_dot(t, perm_ref[...]) for t in ts] for ts in rw] for rw in raw]
        f32 = [[[t.astype(F32) for t in ts] for ts in rw] for rw in raw]
        mean_sq = [[[_dot((t * t).astype(BF16), avg_ref[...]) for t in ts] for ts in fw] for fw in f32]
        kdec = []
        for n, r in enumerate(rows):
            cosf = cos_ref[r, :]
            sinf = sin_ref[r, :]
            gains = ((qg_ref[0:1, :] * cosf, qg_ref[1:2, :] * sinf), (kg_ref[0:1, :] * cosf, kg_ref[1:2, :] * sinf))
            rot = [[(f32[n][a][h] * gains[a][0] + rolled[n][a][h] * gains[a][1])
                    * lax.rsqrt(mean_sq[n][a][h] + EPS) for h in heads] for a in range(2)]
            for h in heads:
                qr_scr[r, qk[h]] = rot[0][h].astype(BF16)
                qdec_scr[r, qk[h]] = (rot[0][h] * qd_ref[h]).astype(BF16)
                kr_scr[r, qk[h]] = rot[1][h].astype(BF16)
            kdec.append([(rot[1][h] * kd_ref[h]).astype(BF16) for h in heads])
        for n, c in enumerate(cs):
            for h in heads:
                kdt_scr[c, h] = _nt_dot(eye_ref[...], kdec[n][h]).astype(BF16)
        return 0

    lax.fori_loop(0, max(n_chunks // RET_PREP_CHUNKS, 1), prep, 0)

    st_scr[...] = jnp.zeros_like(st_scr)
    stb_scr[...] = jnp.zeros_like(stb_scr)

    def state_matmuls(c):
        rows = chunk_rows(c)
        scores = [_nt_dot(qr_scr[rows, qk[h]], kr_scr[rows, qk[h]]) for h in heads]
        cross = [_dot(qdec_scr[rows, qk[h]], stb_scr[h]) for h in heads]
        kv = [_dot(kdt_scr[c, h], v_ref[rows, vv[h]]) for h in heads]
        return scores, cross, kv

    def state_update(slot, scores, cross, kv):
        for h in heads:
            sd_scr[slot, h] = (scores[h] * dec_ref[h]).astype(BF16)
            cr_scr[slot, h] = cross[h]
            st = st_scr[h] * chunk_decay[h] + kv[h]
            st_scr[h] = st
            stb_scr[h] = st.astype(BF16)

    def inner_matmuls(c, slot):
        rows = chunk_rows(c)
        return [_dot(sd_scr[slot, h], v_ref[rows, vv[h]]) for h in heads]

    def write_out(c, slot, inner):
        rows = chunk_rows(c)
        for h in heads:
            o = inner[h] + cr_scr[slot, h]
            on = o * lax.rsqrt(jnp.mean(o * o, axis=-1, keepdims=True) + EPS) * og_ref[:, vv[h]]
            zz = z_ref[rows, vv[h]].astype(F32)
            o_ref[rows, vv[h]] = (on * (zz * _sigmoid(zz))).astype(BF16)

    state_update(0, *state_matmuls(0))

    def step(c, _):
        slot = c % 2
        sm = state_matmuls(c)
        inner = inner_matmuls(c - 1, 1 - slot)
        state_update(slot, *sm)
        write_out(c - 1, 1 - slot, inner)
        return 0

    lax.fori_loop(1, n_chunks, step, 0)
    last = n_chunks - 1
    write_out(last, last % 2, inner_matmuls(last, last % 2))


def _retention(P, cosf, sinf, dec, qd, kd, chunk_decay, qg, kg, og, B, S):
    half = RET_QK_DIM // 2
    lane = jnp.arange(RET_QK_DIM)
    perm = (lane[:, None] == (lane[None, :] + half) % RET_QK_DIM).astype(BF16)
    avg = jnp.full((RET_QK_DIM, RET_QK_DIM), 1.0 / RET_QK_DIM, BF16)
    eye = jnp.eye(RET_QK_DIM, dtype=BF16)
    with_rolled = lambda g: jnp.stack([g, jnp.roll(g, half)])
    const2 = lambda b: (0, 0)
    const3 = lambda b: (0, 0, 0)
    return pl.pallas_call(
        functools.partial(_ret_kernel, chunk_decay),
        out_shape=jax.ShapeDtypeStruct((B * S, RET_V_W), BF16),
        grid=(B,),
        in_specs=[
            pl.BlockSpec((S, RET_QK_W), lambda b: (b, P_RQ)),
            pl.BlockSpec((S, RET_QK_W), lambda b: (b, P_RK)),
            pl.BlockSpec((S, RET_V_W), lambda b: (b, P_RV)),
            pl.BlockSpec((S, RET_V_W), lambda b: (b, P_RZ)),
            pl.BlockSpec((S, RET_QK_DIM), const2),
            pl.BlockSpec((S, RET_QK_DIM), const2),
            pl.BlockSpec((RET_HEADS, RET_CHUNK, RET_CHUNK), const3),
            pl.BlockSpec((RET_HEADS, RET_CHUNK, RET_QK_DIM), const3),
            pl.BlockSpec((RET_HEADS, RET_CHUNK, RET_QK_DIM), const3),
            pl.BlockSpec((2, RET_QK_DIM), const2),
            pl.BlockSpec((2, RET_QK_DIM), const2),
            pl.BlockSpec((1, RET_V_W), const2),
            pl.BlockSpec((RET_QK_DIM, RET_QK_DIM), const2),
            pl.BlockSpec((RET_QK_DIM, RET_QK_DIM), const2),
            pl.BlockSpec((RET_QK_DIM, RET_QK_DIM), const2),
        ],
        out_specs=pl.BlockSpec((S, RET_V_W), lambda b: (b, 0)),
        scratch_shapes=[
            pltpu.VMEM((S, RET_QK_W), BF16),
            pltpu.VMEM((S, RET_QK_W), BF16),
            pltpu.VMEM((S, RET_QK_W), BF16),
            pltpu.VMEM((S // RET_CHUNK, RET_HEADS, RET_QK_DIM, RET_CHUNK), BF16),
            pltpu.VMEM((RET_HEADS, RET_QK_DIM, RET_V_DIM), F32),
            pltpu.VMEM((RET_HEADS, RET_QK_DIM, RET_V_DIM), BF16),
            pltpu.VMEM((2, RET_HEADS, RET_CHUNK, RET_CHUNK), BF16),
            pltpu.VMEM((2, RET_HEADS, RET_CHUNK, RET_V_DIM), F32),
        ],
        compiler_params=pltpu.CompilerParams(
            dimension_semantics=("arbitrary",), vmem_limit_bytes=VMEM_LIMIT),
        name="retention",
    )(P, P, P, P, cosf, sinf, dec, qd, kd, with_rolled(qg), with_rolled(kg), og, perm, avg, eye)


def _sb_kernel(tab_ref, q_ref, k_ref, v_ref, z_ref, qg_ref, kg_ref, tri_ref, avg_ref, o_ref,
               qlo_scr, qhi_scr, kn_scr, vlo_scr, vhi_scr, nlk_scr, s_scr, rs_scr, w_scr, carry_scr, acc_scr):
    S = q_ref.shape[0]
    nc = min(512, S)
    lane_n = lax.broadcasted_iota(jnp.int32, (nc, 2 * SB_HEAD_DIM), 1)
    lo_n = lane_n < SB_HEAD_DIM

    def head_rms(ref, rows, g_ref):
        x = ref[rows, :].astype(F32)
        ms = _dot((x * x).astype(BF16), avg_ref[...])
        return x * lax.rsqrt(ms + EPS) * g_ref[...]

    def norm_body(i, _):
        rows = pl.ds(pl.multiple_of(i * nc, nc), nc)
        qn = head_rms(q_ref, rows, qg_ref) * (SB_HEAD_DIM ** -0.5 * LOG2E)
        qlo_scr[rows, :] = jnp.where(lo_n, qn, 0.0).astype(BF16)
        qhi_scr[rows, :] = jnp.where(lo_n, 0.0, qn).astype(BF16)
        kn_scr[rows, :] = head_rms(k_ref, rows, kg_ref).astype(BF16)
        v = v_ref[rows, :].astype(F32)
        vlo_scr[rows, :] = jnp.where(lo_n, v, 0.0).astype(BF16)
        vhi_scr[rows, :] = jnp.where(lo_n, 0.0, v).astype(BF16)
        return 0

    lax.fori_loop(0, S // nc, norm_body, 0)

    tile = nlk_scr.shape[2]
    n_tiles = S // tile
    n_sub = tile // SB_LANES
    n_diag = n_tiles
    n_off = n_tiles * (n_tiles - 1) // 2
    n_tab = n_diag + n_off
    qi_idx = lax.broadcasted_iota(jnp.int32, (tile, tile), 0)
    ki_idx = lax.broadcasted_iota(jnp.int32, (tile, tile), 1)
    causal = ki_idx < qi_idx

    def tile_rows(idx):
        return pl.ds(pl.multiple_of(idx * tile, tile), tile)

    def stage_scores(n, slot, masked):
        k2 = kn_scr[tile_rows(tab_ref[n_tab + n]), :]
        qrows = tile_rows(tab_ref[n])
        for hh, q_scr in enumerate((qlo_scr, qhi_scr)):
            z = jnp.minimum(_nt_dot(q_scr[qrows, :], k2), SB_MAX_LOGIT)
            nlk = jnp.log(1.0 + jnp.exp2(z)) * LOG2E
            s = z - nlk
            if masked:
                nlk = jnp.where(causal, nlk, 0.0)
                s = jnp.where(causal, s, SB_MASKED_LOGIT)
            nlk_scr[slot, hh] = nlk.astype(BF16)
            s_scr[slot, hh] = s
            rs_scr[slot, hh] = jnp.broadcast_to(jnp.sum(nlk, axis=-1, keepdims=True), (tile, SB_LANES))

    def stage_weights(n, slot, first):
        qi = tab_ref[n]
        for hh in range(2):
            carry = jnp.zeros((tile, SB_LANES), F32) if first else carry_scr[qi, hh]
            after = _dot(nlk_scr[slot, hh], tri_ref[...]) + jnp.concatenate([carry] * n_sub, axis=1)
            w = jnp.exp2(s_scr[slot, hh] + after)
            w_scr[slot, :, hh * tile:(hh + 1) * tile] = w.astype(BF16)
            carry_scr[qi, hh] = carry - rs_scr[slot, hh]

    def stage_values(n, slot, first):
        krows = tile_rows(tab_ref[n_tab + n])
        qi = tab_ref[n]
        vcat = jnp.concatenate([vlo_scr[krows, :], vhi_scr[krows, :]], axis=0)
        contrib = _dot(w_scr[slot], vcat)
        if first:
            acc_scr[qi] = contrib
        else:
            acc_scr[qi] += contrib

    def run_pipelined(n0, count, diag):
        def step(it, parity, a, b, c):
            if a:
                stage_scores(n0 + it, parity, diag)
            if b:
                stage_weights(n0 + it - 1, 1 - parity, diag)
            if c:
                stage_values(n0 + it - 2, parity, diag)

        def static_step(it):
            step(it, it % 2, 0 <= it < count, 0 <= it - 1 < count, 0 <= it - 2 < count)

        steady = list(range(2, count))
        for it in (0, 1):
            static_step(it)
        while len(steady) % SB_UNROLL:
            static_step(steady.pop(0))
        if steady:
            first = steady[0]

            def group(p, _):
                for u in range(SB_UNROLL):
                    step(first + SB_UNROLL * p + u, (first + u) % 2, True, True, True)
                return 0

            lax.fori_loop(0, len(steady) // SB_UNROLL, group, 0)
        for it in sorted({count, count + 1} - {0, 1}):
            static_step(it)

    run_pipelined(0, n_diag, True)
    run_pipelined(n_diag, n_off, False)

    def finish(qi, _):
        qrows = tile_rows(qi)
        zz = z_ref[qrows, :].astype(F32)
        o_ref[qrows, :] = (acc_scr[qi] * (zz * _sigmoid(zz))).astype(BF16)
        return 0

    lax.fori_loop(0, n_tiles, finish, 0)


def _sb_tile_table(n_tiles):
    tiles = [(i, i) for i in range(n_tiles)]
    tiles += [(i, j) for i in range(1, n_tiles) for j in range(i - 1, -1, -1)]
    return jnp.asarray([q for q, _ in tiles] + [k for _, k in tiles], dtype=jnp.int32)


def _stick_breaking(P, qg, kg, tri, B, S):
    n_pairs = SB_HEADS // 2
    pw = 2 * SB_HEAD_DIM
    tile = min(SB_TILE, S)
    n_tiles = S // tile
    head_of_lane = jnp.arange(pw) // SB_HEAD_DIM
    avg = (head_of_lane[:, None] == head_of_lane[None, :]).astype(BF16) * (1.0 / SB_HEAD_DIM)
    const2 = lambda b, p, tab: (0, 0)
    return pl.pallas_call(
        _sb_kernel,
        out_shape=jax.ShapeDtypeStruct((B * S, SB_W), BF16),
        grid_spec=pltpu.PrefetchScalarGridSpec(
            num_scalar_prefetch=1,
            grid=(B, n_pairs),
            in_specs=[
                pl.BlockSpec((S, pw), lambda b, p, tab: (b, P_SQ + p)),
                pl.BlockSpec((S, pw), lambda b, p, tab: (b, P_SK + p)),
                pl.BlockSpec((S, pw), lambda b, p, tab: (b, P_SV + p)),
                pl.BlockSpec((S, pw), lambda b, p, tab: (b, P_SZ + p)),
                pl.BlockSpec((1, pw), const2),
                pl.BlockSpec((1, pw), const2),
                pl.BlockSpec((tile, tile), const2),
                pl.BlockSpec((pw, pw), const2),
            ],
            out_specs=pl.BlockSpec((S, pw), lambda b, p, tab: (b, p)),
            scratch_shapes=[pltpu.VMEM((S, pw), BF16) for _ in range(5)] + [
                pltpu.VMEM((2, 2, tile, tile), BF16),
                pltpu.VMEM((2, 2, tile, tile), F32),
                pltpu.VMEM((2, 2, tile, SB_LANES), F32),
                pltpu.VMEM((2, tile, 2 * tile), BF16),
                pltpu.VMEM((n_tiles, 2, tile, SB_LANES), F32),
                pltpu.VMEM((n_tiles, tile, pw), F32),
            ]),
        compiler_params=pltpu.CompilerParams(
            dimension_semantics=("arbitrary", "arbitrary"), vmem_limit_bytes=VMEM_LIMIT),
        name="stickbreak",
    )(_sb_tile_table(n_tiles), P, P, P, P, qg, kg, tri, avg)


def _ssm_kernel(u_ref, uo_ref, are_ref, aim_ref, bre_ref, bim_ref, cre_ref, cimn_ref, d_ref, wg_ref, bg_ref,
                o_ref, bur0, bur1, bui0, bui1, hr0, hr1, hi0, hi1, cr_scr, ci_scr):
    g = pl.program_id(0)

    @pl.when(g == 0)
    def _():
        for ref in (cr_scr, ci_scr, bur1, bui1, hr0, hi0):
            ref[...] = jnp.zeros_like(ref)

    even = (bur0, bui0, hr0, hi0)
    odd = (bur1, bui1, hr1, hi1)
    refs = (u_ref, uo_ref, are_ref, aim_ref, bre_ref, bim_ref, cre_ref, cimn_ref, d_ref, wg_ref, bg_ref,
            o_ref, cr_scr, ci_scr)

    @pl.when(g % 2 == 0)
    def _():
        _ssm_step(refs, even, odd)

    @pl.when(g % 2 == 1)
    def _():
        _ssm_step(refs, odd, even)


def _ssm_step(refs, new, old):
    (u_ref, uo_ref, are_ref, aim_ref, bre_ref, bim_ref, cre_ref, cimn_ref, d_ref, wg_ref, bg_ref,
     o_ref, cr_scr, ci_scr) = refs
    bur_new, bui_new, hr_new, hi_new = new
    bur_old, bui_old, hr_old, hi_old = old
    T, B, _ = u_ref.shape
    rows_n = T * B
    half_k = SSM_W // 2
    half_n = SSM_STATES // 2

    def scan(lc):
        ls = slice(lc * SSM_LANE_CHUNK, (lc + 1) * SSM_LANE_CHUNK)
        ar = jnp.broadcast_to(are_ref[:, ls], (B, SSM_LANE_CHUNK))
        ai = jnp.broadcast_to(aim_ref[:, ls], (B, SSM_LANE_CHUNK))
        hr = cr_scr[:, ls]
        hi = ci_scr[:, ls]
        for t in range(T):
            rows = slice(t * B, (t + 1) * B)
            hr, hi = (ar * hr - ai * hi + bur_old[rows, ls],
                      ar * hi + ai * hr + bui_old[rows, ls])
            hr_old[rows, ls] = hr.astype(BF16)
            hi_old[rows, ls] = hi.astype(BF16)
        cr_scr[:, ls] = hr
        ci_scr[:, ls] = hi

    lane_chunks = list(range(SSM_STATES // SSM_LANE_CHUNK))

    ys = []
    for c in range(2):
        ss = slice(c * half_n, (c + 1) * half_n)
        ys.append(_dot(hr_new[:, ss], cre_ref[c]) + _dot(hi_new[:, ss], cimn_ref[c]))
        scan(lane_chunks.pop(0))

    u = u_ref[:, :, :SSM_W].reshape(rows_n, SSM_W)
    for c in range(2):
        uc = u[:, c * half_k:(c + 1) * half_k]
        bur_new[:, c * half_n:(c + 1) * half_n] = _dot(uc, bre_ref[c])
        bui_new[:, c * half_n:(c + 1) * half_n] = _dot(uc, bim_ref[c])
        scan(lane_chunks.pop(0))
    assert not lane_chunks

    uo = uo_ref[:, :, :SSM_W].reshape(rows_n, SSM_W).astype(F32)
    y = jnp.concatenate(ys, axis=1) + d_ref[...] * uo
    y = 0.5 * y * (1.0 + jnp.tanh(math.sqrt(2.0 / math.pi) * (y + 0.044715 * (y * y * y))))
    y = y * _sigmoid(_dot(y.astype(BF16), wg_ref[...]) + bg_ref[...])
    cz = uo_ref[:, :, SSM_W:].reshape(rows_n, SSM_W).astype(F32)
    o_ref[...] = (y * (cz * _sigmoid(cz))).astype(BF16).reshape(T, B, SSM_W)


def _ssm(U3, are, aim, bre, bim, cre, cimn, d, wg_all, layer, bg, B, S):
    T = min(SSM_T, S)
    rows_n = T * B
    n = S // T
    const2 = lambda g: (0, 0)
    const3 = lambda g: (0, 0, 0)
    cur = lambda g: (jnp.minimum(g, n - 1), 0, 0)
    done = lambda g: (jnp.clip(g - 2, 0, n - 1), 0, 0)
    return pl.pallas_call(
        _ssm_kernel,
        out_shape=jax.ShapeDtypeStruct((S, B, SSM_W), BF16),
        grid=(n + 2,),
        in_specs=[
            pl.BlockSpec((T, B, COL_TILE), cur),
            pl.BlockSpec((T, B, COL_TILE), done),
            pl.BlockSpec((1, SSM_STATES), const2),
            pl.BlockSpec((1, SSM_STATES), const2),
            pl.BlockSpec((2, SSM_W // 2, SSM_STATES // 2), const3),
            pl.BlockSpec((2, SSM_W // 2, SSM_STATES // 2), const3),
            pl.BlockSpec((2, SSM_STATES // 2, SSM_W // 2), const3),
            pl.BlockSpec((2, SSM_STATES // 2, SSM_W // 2), const3),
            pl.BlockSpec((1, SSM_W), const2),
            pl.BlockSpec((None, SSM_W, SSM_W), lambda g: (layer, 0, 0)),
            pl.BlockSpec((1, SSM_W), const2),
        ],
        out_specs=pl.BlockSpec((T, B, SSM_W), done),
        scratch_shapes=[
            *[pltpu.VMEM((rows_n, SSM_STATES), F32)] * 4,
            *[pltpu.VMEM((rows_n, SSM_STATES), BF16)] * 4,
            *[pltpu.VMEM((B, SSM_STATES), F32)] * 2,
        ],
        compiler_params=pltpu.CompilerParams(
            dimension_semantics=("arbitrary",), vmem_limit_bytes=VMEM_LIMIT),
        name="ssm",
    )(U3, U3, are, aim, bre, bim, cre, cimn, d, wg_all, bg)


def _merge_kernel(ya_ref, yb_ref, yc_ref, ga_ref, gb_ref, gc_ref, x_ref, pa_ref, pb_ref, pc_ref, wo_ref,
                  o_ref):
    tm = x_ref.shape[0]
    mc = min(512, tm)
    for r in range(0, tm, mc):
        rows = slice(r, r + mc)
        m = _sigmoid(ga_ref[rows, :].astype(F32)) * _dot(ya_ref[rows, :], pa_ref[...])
        m += _sigmoid(gb_ref[rows, :].astype(F32)) * _dot(yb_ref[rows, :], pb_ref[...])
        m += _sigmoid(gc_ref[rows, :].astype(F32)) * _dot(yc_ref[rows, :], pc_ref[...])
        o_ref[rows, :] = x_ref[rows, :] + _dot(m.astype(BF16), wo_ref[...])


def _merge(ya, yb, yc2d, P, x2d, pa_all, pb_all, pc_all, wo_all, layer, B, S):
    tm = min(1024, S)
    nt = S // tm
    tok = lambda b, t: (b * nt + t, 0)
    of_layer = lambda b, t: (layer, 0, 0)
    return pl.pallas_call(
        _merge_kernel,
        out_shape=jax.ShapeDtypeStruct((B * S, D_MODEL), F32),
        grid=(B, nt),
        in_specs=[
            pl.BlockSpec((tm, RET_V_W), tok),
            pl.BlockSpec((tm, SB_W), tok),
            pl.BlockSpec((tm, SSM_W), lambda b, t: (t, b)),
            pl.BlockSpec((tm, D_MODEL), lambda b, t: (b * nt + t, P_GA)),
            pl.BlockSpec((tm, D_MODEL), lambda b, t: (b * nt + t, P_GB)),
            pl.BlockSpec((tm, D_MODEL), lambda b, t: (b * nt + t, P_GC)),
            pl.BlockSpec((tm, D_MODEL), tok),
            pl.BlockSpec((None, RET_V_W, D_MODEL), of_layer),
            pl.BlockSpec((None, SB_W, D_MODEL), of_layer),
            pl.BlockSpec((None, SSM_W, D_MODEL), of_layer),
            pl.BlockSpec((None, D_MODEL, D_MODEL), of_layer),
        ],
        out_specs=pl.BlockSpec((tm, D_MODEL), tok),
        compiler_params=pltpu.CompilerParams(
            dimension_semantics=("arbitrary", "arbitrary"), vmem_limit_bytes=VMEM_LIMIT),
        name="merge",
    )(ya, yb, yc2d, P, P, P, x2d, pa_all, pb_all, pc_all, wo_all)


def _rope_tables(S):
    half = RET_QK_DIM // 2
    inv_freq = ROPE_BASE ** (-jnp.arange(half, dtype=F32) / half)
    ang = jnp.arange(S, dtype=F32)[:, None] * inv_freq[None, :]
    cos, sin = jnp.cos(ang), jnp.sin(ang)
    return jnp.concatenate([cos, cos], axis=1), jnp.concatenate([-sin, sin], axis=1)


def _decay_tables():
    log_gamma = jnp.log1p(-jnp.exp2(-5.0 - jnp.arange(RET_HEADS, dtype=F32)))
    idx = jnp.arange(RET_CHUNK, dtype=F32)
    rel = idx[:, None] - idx[None, :]
    dec = jnp.where(rel >= 0, jnp.exp(jnp.maximum(rel, 0.0)[None] * log_gamma[:, None, None]), 0.0)
    kdec = jnp.exp((RET_CHUNK - 1 - idx)[None, :] * log_gamma[:, None])
    qdec = jnp.exp((idx + 1)[None, :] * log_gamma[:, None])
    full = lambda a: jnp.broadcast_to(a[:, :, None], (RET_HEADS, RET_CHUNK, RET_QK_DIM))
    chunk_decay = tuple(float(math.exp(RET_CHUNK * math.log1p(-2.0 ** (-5.0 - h)))) for h in range(RET_HEADS))
    return dec, full(qdec), full(kdec), chunk_decay


def _ssm_tables(a_re, a_im, log_dt, b_re, b_im, c_re, c_im):
    dt = jnp.exp(log_dt)[:, None]
    mag = jnp.exp(dt * a_re)
    ab_re = mag * jnp.cos(dt * a_im)
    ab_im = mag * jnp.sin(dt * a_im)
    den = a_re * a_re + a_im * a_im
    nr = ab_re - 1.0
    coef_re = (nr * a_re + ab_im * a_im) / den
    coef_im = (ab_im * a_re - nr * a_im) / den
    bb_re = coef_re[..., None] * b_re - coef_im[..., None] * b_im
    bb_im = coef_re[..., None] * b_im + coef_im[..., None] * b_re
    gl = SSM_GROUPS // 2
    eye = jnp.eye(gl, dtype=F32)

    def b_blocks(bb):
        return jnp.einsum('cgpm,gh->cgmhp', bb.reshape(2, gl, SSM_STATE, SSM_GROUP), eye).reshape(
            2, gl * SSM_GROUP, gl * SSM_STATE).astype(BF16)

    def c_blocks(cc):
        return jnp.einsum('cgmp,gh->cgphm', cc.reshape(2, gl, SSM_GROUP, SSM_STATE), eye).reshape(
            2, gl * SSM_STATE, gl * SSM_GROUP).astype(BF16)

    return (ab_re.reshape(1, SSM_STATES), ab_im.reshape(1, SSM_STATES),
            b_blocks(bb_re), b_blocks(bb_im), c_blocks(c_re), c_blocks(-c_im))


def _suffix_sum_matrix(tile):
    k = jnp.arange(tile)
    return -(k[:, None] > k[None, :]).astype(BF16)


def kernel(x, norm_g, w_in, ret_q_norm, ret_k_norm, ret_out_norm, sb_q_norm, sb_k_norm, ssm_a_re, ssm_a_im,
           ssm_log_dt, ssm_b_re, ssm_b_im, ssm_c_re, ssm_c_im, ssm_d, ssm_w_glu, ssm_b_glu, proj_a, proj_b,
           proj_c, w_out):
    B, S, D = x.shape
    depth = w_in.shape[0]
    assert D == D_MODEL and B == 16 and S % 128 == 0
    cosf, sinf = _rope_tables(S)
    dec, qd, kd, chunk_decay = _decay_tables()
    tri = _suffix_sum_matrix(min(SB_TILE, S))
    x2d = x.reshape(B * S, D)
    w_in_b, w_glu_b, proj_a_b, proj_b_b, proj_c_b, w_out_b = (
        w.astype(BF16) for w in (w_in, ssm_w_glu, proj_a, proj_b, proj_c, w_out))
    for l in range(depth):
        P, U = _inproj(x2d, norm_g[l].reshape(1, D), w_in_b, l, B, S)
        ya = _retention(P, cosf, sinf, dec, qd, kd, chunk_decay,
                        ret_q_norm[l], ret_k_norm[l] * (RET_QK_DIM ** -0.5),
                        ret_out_norm[l].reshape(1, -1), B, S)
        yb = _stick_breaking(P, jnp.tile(sb_q_norm[l], 2).reshape(1, -1),
                             jnp.tile(sb_k_norm[l], 2).reshape(1, -1), tri, B, S)
        are, aim, bre, bim, cre, cimn = _ssm_tables(ssm_a_re[l], ssm_a_im[l], ssm_log_dt[l], ssm_b_re[l],
                                                    ssm_b_im[l], ssm_c_re[l], ssm_c_im[l])
        yc = _ssm(U.reshape(S, B, COL_TILE), are, aim, bre, bim, cre, cimn,
                  ssm_d[l].reshape(1, -1), w_glu_b, l, ssm_b_glu[l].reshape(1, -1), B, S)
        x2d = _merge(ya, yb, yc.reshape(S, B * SSM_W), P, x2d, proj_a_b, proj_b_b, proj_c_b, w_out_b, l, B, S)
    return x2d.reshape(B, S, D)
```

```python
import functools
import math

import jax
import jax.numpy as jnp
from jax import lax
from jax.experimental import pallas as pl
from jax.experimental.pallas import tpu as pltpu

F32 = jnp.float32
BF16 = jnp.bfloat16

D_MODEL = 1024
RET_HEADS = 4
RET_QK_W = 512
RET_V_W = 1024
RET_QK_DIM = 128
RET_V_DIM = 256
RET_CHUNK = 128
RET_PREP_CHUNKS = 2
ROPE_BASE = 10000.0
SB_HEADS = 8
SB_W = 512
SB_HEAD_DIM = 64
SB_TILE = 256
SB_LANES = 128
SB_MASKED_LOGIT = -1e30
SB_MAX_LOGIT = 126.0
SB_UNROLL = 4
LOG2E = math.log2(math.e)
SSM_W = 512
SSM_GROUP = 16
SSM_GROUPS = 32
SSM_STATE = 64
SSM_STATES = SSM_GROUPS * SSM_STATE
EPS = 1e-6
IN_W = 9216

COL_TILE = 1024
INPROJ_ROWS = 2048
MERGE_ROWS = 512
SSM_COL_TILE = 5
P_W = IN_W - COL_TILE
P_RQ, P_RK = 0, 1
P_RV, P_RZ = 1, 2
P_SQ, P_SK, P_SV, P_SZ = 24, 28, 32, 36
P_GA, P_GB, P_GC = 5, 6, 7

SSM_T = 32
SSM_LANE_CHUNK = 512
VMEM_LIMIT = 56 * 1024 * 1024


def _sigmoid(v):
    return 1.0 / (1.0 + jnp.exp(-v))


def _nt_dot(a, b):
    return lax.dot_general(a, b, (((1,), (1,)), ((), ())), preferred_element_type=F32)


def _dot(a, b):
    return jnp.dot(a, b, preferred_element_type=F32)


def _inproj_kernel(x_ref, g_ref, w_ref, p_ref, u_ref, h_scr):
    j = pl.program_id(2)
    tm = x_ref.shape[0]
    rc = min(256, tm)

    @pl.when(j == 0)
    def _():
        def body(i, _):
            rows = pl.ds(pl.multiple_of(i * rc, rc), rc)
            x = x_ref[rows, :]
            ms = jnp.mean(x * x, axis=-1, keepdims=True)
            h_scr[rows, :] = (x * lax.rsqrt(ms + EPS) * g_ref[...]).astype(BF16)
            return 0
        lax.fori_loop(0, tm // rc, body, 0)

    mc = tm

    @pl.when(j != SSM_COL_TILE)
    def _():
        for r in range(0, tm, mc):
            p_ref[r:r + mc, :] = _dot(h_scr[r:r + mc, :], w_ref[...]).astype(BF16)

    @pl.when(j == SSM_COL_TILE)
    def _():
        for r in range(0, tm, mc):
            u_ref[r:r + mc, :] = _dot(h_scr[r:r + mc, :], w_ref[...]).astype(BF16)


def _inproj(x2d, g, w_all, layer, B, S):
    tm = min(INPROJ_ROWS, S)
    nt = S // tm
    n_col = IN_W // COL_TILE
    return pl.pallas_call(
        _inproj_kernel,
        out_shape=(jax.ShapeDtypeStruct((B * S, P_W), BF16),
                   jax.ShapeDtypeStruct((S, B * COL_TILE), BF16)),
        grid=(B, nt, n_col),
        in_specs=[
            pl.BlockSpec((tm, D_MODEL), lambda b, t, j: (b * nt + t, 0)),
            pl.BlockSpec((1, D_MODEL), lambda b, t, j: (0, 0)),
            pl.BlockSpec((None, D_MODEL, COL_TILE), lambda b, t, j: (layer, 0, j)),
        ],
        out_specs=(
            pl.BlockSpec((tm, COL_TILE),
                         lambda b, t, j: (b * nt + t, j - (j >= SSM_COL_TILE).astype(jnp.int32))),
            pl.BlockSpec((tm, COL_TILE), lambda b, t, j: (t, b)),
        ),
        scratch_shapes=[pltpu.VMEM((tm, D_MODEL), BF16)],
        compiler_params=pltpu.CompilerParams(
            dimension_semantics=("arbitrary", "arbitrary", "arbitrary"),
            vmem_limit_bytes=VMEM_LIMIT),
        name="inproj",
    )(x2d, g, w_all)


def _ret_kernel(chunk_decay, q_ref, k_ref, v_ref, z_ref, cos_ref, sin_ref, dec_ref, qd_ref, kd_ref,
                qg_ref, kg_ref, og_ref, perm_ref, avg_ref, eye_ref, o_ref,
                qr_scr, qdec_scr, kr_scr, kdt_scr, st_scr, stb_scr, sd_scr):
    S = q_ref.shape[0]
    n_chunks = S // RET_CHUNK
    heads = range(RET_HEADS)
    qk = [slice(h * RET_QK_DIM, (h + 1) * RET_QK_DIM) for h in heads]
    vv = [slice(h * RET_V_DIM, (h + 1) * RET_V_DIM) for h in heads]

    def chunk_rows(c):
        return pl.ds(pl.multiple_of(c * RET_CHUNK, RET_CHUNK), RET_CHUNK)

    def prep(i, _):
        cs = [i * RET_PREP_CHUNKS + u for u in range(min(RET_PREP_CHUNKS, n_chunks))]
        rows = [chunk_rows(c) for c in cs]
        raw = [[[ref[r, qk[h]] for h in heads] for ref in (q_ref, k_ref)] for r in rows]
        rolled = [[[_dot(t, perm_ref[...]) for t in ts] for ts in rw] for rw in raw]
        f32 = [[[t.astype(F32) for t in ts] for ts in rw] for rw in raw]
        mean_sq = [[[_dot((t * t).astype(BF16), avg_ref[...]) for t in ts] for ts in fw] for fw in f32]
        kdec = []
        for n, r in enumerate(rows):
            cosf = cos_ref[r, :]
            sinf = sin_ref[r, :]
            gains = ((qg_ref[0:1, :] * cosf, qg_ref[1:2, :] * sinf), (kg_ref[0:1, :] * cosf, kg_ref[1:2, :] * sinf))
            rot = [[(f32[n][a][h] * gains[a][0] + rolled[n][a][h] * gains[a][1])
                    * lax.rsqrt(mean_sq[n][a][h] + EPS) for h in heads] for a in range(2)]
            for h in heads:
                qr_scr[r, qk[h]] = rot[0][h].astype(BF16)
                qdec_scr[r, qk[h]] = (rot[0][h] * qd_ref[h]).astype(BF16)
                kr_scr[r, qk[h]] = rot[1][h].astype(BF16)
            kdec.append([(rot[1][h] * kd_ref[h]).astype(BF16) for h in heads])
        for n, c in enumerate(cs):
            for h in heads:
                kdt_scr[c, h] = _nt_dot(eye_ref[...], kdec[n][h]).astype(BF16)
        return 0

    lax.fori_loop(0, max(n_chunks // RET_PREP_CHUNKS, 1), prep, 0)

    st_scr[...] = jnp.zeros_like(st_scr)
    stb_scr[...] = jnp.zeros_like(stb_scr)

    def state_matmuls(c):
        rows = chunk_rows(c)
        scores = [_nt_dot(qr_scr[rows, qk[h]], kr_scr[rows, qk[h]]) for h in heads]
        kv = [_dot(kdt_scr[c, h], v_ref[rows, vv[h]]) for h in heads]
        return scores, kv

    def state_update(slot, scores, kv):
        for h in heads:
            sd_scr[slot, h] = (scores[h] * dec_ref[h]).astype(BF16)
            st = st_scr[h] * chunk_decay[h] + kv[h]
            st_scr[h] = st
            stb_scr[slot, h] = st.astype(BF16)

    def out_matmuls(c, slot):
        rows = chunk_rows(c)
        return [_dot(jnp.concatenate([sd_scr[slot, h], qdec_scr[rows, qk[h]]], axis=1),
                     jnp.concatenate([v_ref[rows, vv[h]], stb_scr[1 - slot, h]], axis=0)) for h in heads]

    def write_out(c, outs):
        rows = chunk_rows(c)
        for h in heads:
            o = outs[h]
            on = o * lax.rsqrt(jnp.mean(o * o, axis=-1, keepdims=True) + EPS) * og_ref[:, vv[h]]
            zz = z_ref[rows, vv[h]].astype(F32)
            o_ref[rows, vv[h]] = (on * (zz * _sigmoid(zz))).astype(BF16)

    state_update(0, *state_matmuls(0))

    def step(c, _):
        slot = c % 2
        sm = state_matmuls(c)
        outs = out_matmuls(c - 1, 1 - slot)
        state_update(slot, *sm)
        write_out(c - 1, outs)
        return 0

    lax.fori_loop(1, n_chunks, step, 0)
    last = n_chunks - 1
    write_out(last, out_matmuls(last, last % 2))


def _retention(P, cosf, sinf, dec, qd, kd, chunk_decay, qg, kg, og, B, S):
    half = RET_QK_DIM // 2
    lane = jnp.arange(RET_QK_DIM)
    perm = (lane[:, None] == (lane[None, :] + half) % RET_QK_DIM).astype(BF16)
    avg = jnp.full((RET_QK_DIM, RET_QK_DIM), 1.0 / RET_QK_DIM, BF16)
    eye = jnp.eye(RET_QK_DIM, dtype=BF16)
    with_rolled = lambda g: jnp.stack([g, jnp.roll(g, half)])
    const2 = lambda b: (0, 0)
    const3 = lambda b: (0, 0, 0)
    return pl.pallas_call(
        functools.partial(_ret_kernel, chunk_decay),
        out_shape=jax.ShapeDtypeStruct((B * S, RET_V_W), BF16),
        grid=(B,),
        in_specs=[
            pl.BlockSpec((S, RET_QK_W), lambda b: (b, P_RQ)),
            pl.BlockSpec((S, RET_QK_W), lambda b: (b, P_RK)),
            pl.BlockSpec((S, RET_V_W), lambda b: (b, P_RV)),
            pl.BlockSpec((S, RET_V_W), lambda b: (b, P_RZ)),
            pl.BlockSpec((S, RET_QK_DIM), const2),
            pl.BlockSpec((S, RET_QK_DIM), const2),
            pl.BlockSpec((RET_HEADS, RET_CHUNK, RET_CHUNK), const3),
            pl.BlockSpec((RET_HEADS, RET_CHUNK, RET_QK_DIM), const3),
            pl.BlockSpec((RET_HEADS, RET_CHUNK, RET_QK_DIM), const3),
            pl.BlockSpec((2, RET_QK_DIM), const2),
            pl.BlockSpec((2, RET_QK_DIM), const2),
            pl.BlockSpec((1, RET_V_W), const2),
            pl.BlockSpec((RET_QK_DIM, RET_QK_DIM), const2),
            pl.BlockSpec((RET_QK_DIM, RET_QK_DIM), const2),
            pl.BlockSpec((RET_QK_DIM, RET_QK_DIM), const2),
        ],
        out_specs=pl.BlockSpec((S, RET_V_W), lambda b: (b, 0)),
        scratch_shapes=[
            pltpu.VMEM((S, RET_QK_W), BF16),
            pltpu.VMEM((S, RET_QK_W), BF16),
            pltpu.VMEM((S, RET_QK_W), BF16),
            pltpu.VMEM((S // RET_CHUNK, RET_HEADS, RET_QK_DIM, RET_CHUNK), BF16),
            pltpu.VMEM((RET_HEADS, RET_QK_DIM, RET_V_DIM), F32),
            pltpu.VMEM((2, RET_HEADS, RET_QK_DIM, RET_V_DIM), BF16),
            pltpu.VMEM((2, RET_HEADS, RET_CHUNK, RET_CHUNK), BF16),
        ],
        compiler_params=pltpu.CompilerParams(
            dimension_semantics=("arbitrary",), vmem_limit_bytes=VMEM_LIMIT),
        name="retention",
    )(P, P, P, P, cosf, sinf, dec, qd, kd, with_rolled(qg), with_rolled(kg), og, perm, avg, eye)


def _sb_kernel(tab_ref, q_ref, k_ref, v_ref, z_ref, qg_ref, kg_ref, tri_ref, avg_ref, o_ref,
               qlo_scr, qhi_scr, kn_scr, vlo_scr, vhi_scr, nlk_scr, s_scr, rs_scr, w_scr, carry_scr, acc_scr):
    S = q_ref.shape[0]
    nc = min(512, S)
    lane_n = lax.broadcasted_iota(jnp.int32, (nc, 2 * SB_HEAD_DIM), 1)
    lo_n = lane_n < SB_HEAD_DIM

    def head_rms(ref, rows, g_ref):
        x = ref[rows, :].astype(F32)
        ms = _dot((x * x).astype(BF16), avg_ref[...])
        return x * lax.rsqrt(ms + EPS) * g_ref[...]

    def norm_body(i, _):
        rows = pl.ds(pl.multiple_of(i * nc, nc), nc)
        qn = head_rms(q_ref, rows, qg_ref) * (SB_HEAD_DIM ** -0.5 * LOG2E)
        qlo_scr[rows, :] = jnp.where(lo_n, qn, 0.0).astype(BF16)
        qhi_scr[rows, :] = jnp.where(lo_n, 0.0, qn).astype(BF16)
        kn_scr[rows, :] = head_rms(k_ref, rows, kg_ref).astype(BF16)
        v = v_ref[rows, :].astype(F32)
        vlo_scr[rows, :] = jnp.where(lo_n, v, 0.0).astype(BF16)
        vhi_scr[rows, :] = jnp.where(lo_n, 0.0, v).astype(BF16)
        return 0

    lax.fori_loop(0, S // nc, norm_body, 0)

    tile = nlk_scr.shape[2]
    n_tiles = S // tile
    n_sub = tile // SB_LANES
    n_diag = n_tiles
    n_off = n_tiles * (n_tiles - 1) // 2
    n_tab = n_diag + n_off
    qi_idx = lax.broadcasted_iota(jnp.int32, (tile, tile), 0)
    ki_idx = lax.broadcasted_iota(jnp.int32, (tile, tile), 1)
    causal = ki_idx < qi_idx

    def tile_rows(idx):
        return pl.ds(pl.multiple_of(idx * tile, tile), tile)

    def stage_scores(n, slot, masked):
        k2 = kn_scr[tile_rows(tab_ref[n_tab + n]), :]
        qrows = tile_rows(tab_ref[n])
        for hh, q_scr in enumerate((qlo_scr, qhi_scr)):
            z = jnp.minimum(_nt_dot(q_scr[qrows, :], k2), SB_MAX_LOGIT)
            nlk = jnp.log(1.0 + jnp.exp2(z)) * LOG2E
            s = z - nlk
            if masked:
                nlk = jnp.where(causal, nlk, 0.0)
                s = jnp.where(causal, s, SB_MASKED_LOGIT)
            nlk_scr[slot, hh] = nlk.astype(BF16)
            s_scr[slot, hh] = s
            rs_scr[slot, hh] = jnp.broadcast_to(jnp.sum(nlk, axis=-1, keepdims=True), (tile, SB_LANES))

    def stage_weights(n, slot, first):
        qi = tab_ref[n]
        for hh in range(2):
            carry = jnp.zeros((tile, SB_LANES), F32) if first else carry_scr[qi, hh]
            after = _dot(nlk_scr[slot, hh], tri_ref[...]) + jnp.concatenate([carry] * n_sub, axis=1)
            w = jnp.exp2(s_scr[slot, hh] + after)
            w_scr[slot, :, hh * tile:(hh + 1) * tile] = w.astype(BF16)
            carry_scr[qi, hh] = carry - rs_scr[slot, hh]

    def stage_values(n, slot, first):
        krows = tile_rows(tab_ref[n_tab + n])
        qi = tab_ref[n]
        vcat = jnp.concatenate([vlo_scr[krows, :], vhi_scr[krows, :]], axis=0)
        contrib = _dot(w_scr[slot], vcat)
        if first:
            acc_scr[qi] = contrib
        else:
            acc_scr[qi] += contrib

    def run_pipelined(n0, count, diag):
        def step(it, parity, a, b, c):
            if a:
                stage_scores(n0 + it, parity, diag)
            if b:
                stage_weights(n0 + it - 1, 1 - parity, diag)
            if c:
                stage_values(n0 + it - 2, parity, diag)

        def static_step(it):
            step(it, it % 2, 0 <= it < count, 0 <= it - 1 < count, 0 <= it - 2 < count)

        steady = list(range(2, count))
        for it in (0, 1):
            static_step(it)
        while len(steady) % SB_UNROLL:
            static_step(steady.pop(0))
        if steady:
            first = steady[0]

            def group(p, _):
                for u in range(SB_UNROLL):
                    step(first + SB_UNROLL * p + u, (first + u) % 2, True, True, True)
                return 0

            lax.fori_loop(0, len(steady) // SB_UNROLL, group, 0)
        for it in sorted({count, count + 1} - {0, 1}):
            static_step(it)

    run_pipelined(0, n_diag, True)
    run_pipelined(n_diag, n_off, False)

    def finish(qi, _):
        qrows = tile_rows(qi)
        zz = z_ref[qrows, :].astype(F32)
        o_ref[qrows, :] = (acc_scr[qi] * (zz * _sigmoid(zz))).astype(BF16)
        return 0

    lax.fori_loop(0, n_tiles, finish, 0)


def _sb_tile_table(n_tiles):
    tiles = [(i, i) for i in range(n_tiles)]
    tiles += [(i, j) for i in range(1, n_tiles) for j in range(i - 1, -1, -1)]
    return jnp.asarray([q for q, _ in tiles] + [k for _, k in tiles], dtype=jnp.int32)


def _stick_breaking(P, qg, kg, tri, B, S):
    n_pairs = SB_HEADS // 2
    pw = 2 * SB_HEAD_DIM
    tile = min(SB_TILE, S)
    n_tiles = S // tile
    head_of_lane = jnp.arange(pw) // SB_HEAD_DIM
    avg = (head_of_lane[:, None] == head_of_lane[None, :]).astype(BF16) * (1.0 / SB_HEAD_DIM)
    const2 = lambda b, p, tab: (0, 0)
    return pl.pallas_call(
        _sb_kernel,
        out_shape=jax.ShapeDtypeStruct((B * S, SB_W), BF16),
        grid_spec=pltpu.PrefetchScalarGridSpec(
            num_scalar_prefetch=1,
            grid=(B, n_pairs),
            in_specs=[
                pl.BlockSpec((S, pw), lambda b, p, tab: (b, P_SQ + p)),
                pl.BlockSpec((S, pw), lambda b, p, tab: (b, P_SK + p)),
                pl.BlockSpec((S, pw), lambda b, p, tab: (b, P_SV + p)),
                pl.BlockSpec((S, pw), lambda b, p, tab: (b, P_SZ + p)),
                pl.BlockSpec((1, pw), const2),
                pl.BlockSpec((1, pw), const2),
                pl.BlockSpec((tile, tile), const2),
                pl.BlockSpec((pw, pw), const2),
            ],
            out_specs=pl.BlockSpec((S, pw), lambda b, p, tab: (b, p)),
            scratch_shapes=[pltpu.VMEM((S, pw), BF16) for _ in range(5)] + [
                pltpu.VMEM((2, 2, tile, tile), BF16),
                pltpu.VMEM((2, 2, tile, tile), F32),
                pltpu.VMEM((2, 2, tile, SB_LANES), F32),
                pltpu.VMEM((2, tile, 2 * tile), BF16),
                pltpu.VMEM((n_tiles, 2, tile, SB_LANES), F32),
                pltpu.VMEM((n_tiles, tile, pw), F32),
            ]),
        compiler_params=pltpu.CompilerParams(
            dimension_semantics=("arbitrary", "arbitrary"), vmem_limit_bytes=VMEM_LIMIT),
        name="stickbreak",
    )(_sb_tile_table(n_tiles), P, P, P, P, qg, kg, tri, avg)


def _ssm_kernel(u_ref, uo_ref, are_ref, aim_ref, bre_ref, bim_ref, cre_ref, cimn_ref, d_ref, wg_ref, bg_ref,
                o_ref, bur0, bur1, bui0, bui1, hr0, hr1, hi0, hi1, cr_scr, ci_scr):
    g = pl.program_id(0)

    @pl.when(g == 0)
    def _():
        for ref in (cr_scr, ci_scr, bur1, bui1, hr0, hi0):
            ref[...] = jnp.zeros_like(ref)

    even = (bur0, bui0, hr0, hi0)
    odd = (bur1, bui1, hr1, hi1)
    refs = (u_ref, uo_ref, are_ref, aim_ref, bre_ref, bim_ref, cre_ref, cimn_ref, d_ref, wg_ref, bg_ref,
            o_ref, cr_scr, ci_scr)

    @pl.when(g % 2 == 0)
    def _():
        _ssm_step(refs, even, odd)

    @pl.when(g % 2 == 1)
    def _():
        _ssm_step(refs, odd, even)


def _ssm_step(refs, new, old):
    (u_ref, uo_ref, are_ref, aim_ref, bre_ref, bim_ref, cre_ref, cimn_ref, d_ref, wg_ref, bg_ref,
     o_ref, cr_scr, ci_scr) = refs
    bur_new, bui_new, hr_new, hi_new = new
    bur_old, bui_old, hr_old, hi_old = old
    T, B, _ = u_ref.shape
    rows_n = T * B
    half_k = SSM_W // 2
    half_n = SSM_STATES // 2

    def scan(lc):
        ls = slice(lc * SSM_LANE_CHUNK, (lc + 1) * SSM_LANE_CHUNK)
        ar = jnp.broadcast_to(are_ref[:, ls], (B, SSM_LANE_CHUNK))
        ai = jnp.broadcast_to(aim_ref[:, ls], (B, SSM_LANE_CHUNK))
        hr = cr_scr[:, ls]
        hi = ci_scr[:, ls]
        for t in range(T):
            rows = slice(t * B, (t + 1) * B)
            hr, hi = (ar * hr - ai * hi + bur_old[rows, ls],
                      ar * hi + ai * hr + bui_old[rows, ls])
            hr_old[rows, ls] = hr.astype(BF16)
            hi_old[rows, ls] = hi.astype(BF16)
        cr_scr[:, ls] = hr
        ci_scr[:, ls] = hi

    lane_chunks = list(range(SSM_STATES // SSM_LANE_CHUNK))

    ys = []
    for c in range(2):
        ss = slice(c * half_n, (c + 1) * half_n)
        ys.append(_dot(hr_new[:, ss], cre_ref[c]) + _dot(hi_new[:, ss], cimn_ref[c]))
        scan(lane_chunks.pop(0))

    u = u_ref[:, :, :SSM_W].reshape(rows_n, SSM_W)
    for c in range(2):
        uc = u[:, c * half_k:(c + 1) * half_k]
        bur_new[:, c * half_n:(c + 1) * half_n] = _dot(uc, bre_ref[c])
        bui_new[:, c * half_n:(c + 1) * half_n] = _dot(uc, bim_ref[c])
        scan(lane_chunks.pop(0))
    assert not lane_chunks

    uo = uo_ref[:, :, :SSM_W].reshape(rows_n, SSM_W).astype(F32)
    y = jnp.concatenate(ys, axis=1) + d_ref[...] * uo
    y = 0.5 * y * (1.0 + jnp.tanh(math.sqrt(2.0 / math.pi) * (y + 0.044715 * (y * y * y))))
    y = y * _sigmoid(_dot(y.astype(BF16), wg_ref[...]) + bg_ref[...])
    cz = uo_ref[:, :, SSM_W:].reshape(rows_n, SSM_W).astype(F32)
    o_ref[...] = (y * (cz * _sigmoid(cz))).astype(BF16).reshape(T, B, SSM_W)


def _ssm(U3, are, aim, bre, bim, cre, cimn, d, wg_all, layer, bg, B, S):
    T = min(SSM_T, S)
    rows_n = T * B
    n = S // T
    const2 = lambda g: (0, 0)
    const3 = lambda g: (0, 0, 0)
    cur = lambda g: (jnp.minimum(g, n - 1), 0, 0)
    done = lambda g: (jnp.clip(g - 2, 0, n - 1), 0, 0)
    return pl.pallas_call(
        _ssm_kernel,
        out_shape=jax.ShapeDtypeStruct((S, B, SSM_W), BF16),
        grid=(n + 2,),
        in_specs=[
            pl.BlockSpec((T, B, COL_TILE), cur),
            pl.BlockSpec((T, B, COL_TILE), done),
            pl.BlockSpec((1, SSM_STATES), const2),
            pl.BlockSpec((1, SSM_STATES), const2),
            pl.BlockSpec((2, SSM_W // 2, SSM_STATES // 2), const3),
            pl.BlockSpec((2, SSM_W // 2, SSM_STATES // 2), const3),
            pl.BlockSpec((2, SSM_STATES // 2, SSM_W // 2), const3),
            pl.BlockSpec((2, SSM_STATES // 2, SSM_W // 2), const3),
            pl.BlockSpec((1, SSM_W), const2),
            pl.BlockSpec((None, SSM_W, SSM_W), lambda g: (layer, 0, 0)),
            pl.BlockSpec((1, SSM_W), const2),
        ],
        out_specs=pl.BlockSpec((T, B, SSM_W), done),
        scratch_shapes=[
            *[pltpu.VMEM((rows_n, SSM_STATES), F32)] * 4,
            *[pltpu.VMEM((rows_n, SSM_STATES), BF16)] * 4,
            *[pltpu.VMEM((B, SSM_STATES), F32)] * 2,
        ],
        compiler_params=pltpu.CompilerParams(
            dimension_semantics=("arbitrary",), vmem_limit_bytes=VMEM_LIMIT),
        name="ssm",
    )(U3, U3, are, aim, bre, bim, cre, cimn, d, wg_all, bg)


def _merge_kernel(ya_ref, yb_ref, yc_ref, ga_ref, gb_ref, gc_ref, x_ref, pa_ref, pb_ref, pc_ref, wo_ref,
                  o_ref):
    tm = x_ref.shape[0]
    mc = min(MERGE_ROWS, tm)
    for r in range(0, tm, mc):
        rows = slice(r, r + mc)
        m = _sigmoid(ga_ref[rows, :].astype(F32)) * _dot(ya_ref[rows, :], pa_ref[...])
        m += _sigmoid(gb_ref[rows, :].astype(F32)) * _dot(yb_ref[rows, :], pb_ref[...])
        m += _sigmoid(gc_ref[rows, :].astype(F32)) * _dot(yc_ref[rows, :], pc_ref[...])
        o_ref[rows, :] = x_ref[rows, :] + _dot(m.astype(BF16), wo_ref[...])


def _merge(ya, yb, yc2d, P, x2d, pa_all, pb_all, pc_all, wo_all, layer, B, S):
    tm = min(1024, S)
    nt = S // tm
    tok = lambda b, t: (b * nt + t, 0)
    of_layer = lambda b, t: (layer, 0, 0)
    return pl.pallas_call(
        _merge_kernel,
        out_shape=jax.ShapeDtypeStruct((B * S, D_MODEL), F32),
        grid=(B, nt),
        in_specs=[
            pl.BlockSpec((tm, RET_V_W), tok),
            pl.BlockSpec((tm, SB_W), tok),
            pl.BlockSpec((tm, SSM_W), lambda b, t: (t, b)),
            pl.BlockSpec((tm, D_MODEL), lambda b, t: (b * nt + t, P_GA)),
            pl.BlockSpec((tm, D_MODEL), lambda b, t: (b * nt + t, P_GB)),
            pl.BlockSpec((tm, D_MODEL), lambda b, t: (b * nt + t, P_GC)),
            pl.BlockSpec((tm, D_MODEL), tok),
            pl.BlockSpec((None, RET_V_W, D_MODEL), of_layer),
            pl.BlockSpec((None, SB_W, D_MODEL), of_layer),
            pl.BlockSpec((None, SSM_W, D_MODEL), of_layer),
            pl.BlockSpec((None, D_MODEL, D_MODEL), of_layer),
        ],
        out_specs=pl.BlockSpec((tm, D_MODEL), tok),
        compiler_params=pltpu.CompilerParams(
            dimension_semantics=("arbitrary", "arbitrary"), vmem_limit_bytes=VMEM_LIMIT),
        name="merge",
    )(ya, yb, yc2d, P, P, P, x2d, pa_all, pb_all, pc_all, wo_all)


def _rope_tables(S):
    half = RET_QK_DIM // 2
    inv_freq = ROPE_BASE ** (-jnp.arange(half, dtype=F32) / half)
    ang = jnp.arange(S, dtype=F32)[:, None] * inv_freq[None, :]
    cos, sin = jnp.cos(ang), jnp.sin(ang)
    return jnp.concatenate([cos, cos], axis=1), jnp.concatenate([-sin, sin], axis=1)


def _decay_tables():
    log_gamma = jnp.log1p(-jnp.exp2(-5.0 - jnp.arange(RET_HEADS, dtype=F32)))
    idx = jnp.arange(RET_CHUNK, dtype=F32)
    rel = idx[:, None] - idx[None, :]
    dec = jnp.where(rel >= 0, jnp.exp(jnp.maximum(rel, 0.0)[None] * log_gamma[:, None, None]), 0.0)
    kdec = jnp.exp((RET_CHUNK - 1 - idx)[None, :] * log_gamma[:, None])
    qdec = jnp.exp((idx + 1)[None, :] * log_gamma[:, None])
    full = lambda a: jnp.broadcast_to(a[:, :, None], (RET_HEADS, RET_CHUNK, RET_QK_DIM))
    chunk_decay = tuple(float(math.exp(RET_CHUNK * math.log1p(-2.0 ** (-5.0 - h)))) for h in range(RET_HEADS))
    return dec, full(qdec), full(kdec), chunk_decay


def _ssm_tables(a_re, a_im, log_dt, b_re, b_im, c_re, c_im):
    dt = jnp.exp(log_dt)[:, None]
    mag = jnp.exp(dt * a_re)
    ab_re = mag * jnp.cos(dt * a_im)
    ab_im = mag * jnp.sin(dt * a_im)
    den = a_re * a_re + a_im * a_im
    nr = ab_re - 1.0
    coef_re = (nr * a_re + ab_im * a_im) / den
    coef_im = (ab_im * a_re - nr * a_im) / den
    bb_re = coef_re[..., None] * b_re - coef_im[..., None] * b_im
    bb_im = coef_re[..., None] * b_im + coef_im[..., None] * b_re
    gl = SSM_GROUPS // 2
    eye = jnp.eye(gl, dtype=F32)

    def b_blocks(bb):
        return jnp.einsum('cgpm,gh->cgmhp', bb.reshape(2, gl, SSM_STATE, SSM_GROUP), eye).reshape(
            2, gl * SSM_GROUP, gl * SSM_STATE).astype(BF16)

    def c_blocks(cc):
        return jnp.einsum('cgmp,gh->cgphm', cc.reshape(2, gl, SSM_GROUP, SSM_STATE), eye).reshape(
            2, gl * SSM_STATE, gl * SSM_GROUP).astype(BF16)

    return (ab_re.reshape(1, SSM_STATES), ab_im.reshape(1, SSM_STATES),
            b_blocks(bb_re), b_blocks(bb_im), c_blocks(c_re), c_blocks(-c_im))


def _suffix_sum_matrix(tile):
    k = jnp.arange(tile)
    return -(k[:, None] > k[None, :]).astype(BF16)


def kernel(x, norm_g, w_in, ret_q_norm, ret_k_norm, ret_out_norm, sb_q_norm, sb_k_norm, ssm_a_re, ssm_a_im,
           ssm_log_dt, ssm_b_re, ssm_b_im, ssm_c_re, ssm_c_im, ssm_d, ssm_w_glu, ssm_b_glu, proj_a, proj_b,
           proj_c, w_out):
    B, S, D = x.shape
    depth = w_in.shape[0]
    assert D == D_MODEL and B == 16 and S % 128 == 0
    cosf, sinf = _rope_tables(S)
    dec, qd, kd, chunk_decay = _decay_tables()
    tri = _suffix_sum_matrix(min(SB_TILE, S))
    x2d = x.reshape(B * S, D)
    w_in_b, w_glu_b, proj_a_b, proj_b_b, proj_c_b, w_out_b = (
        w.astype(BF16) for w in (w_in, ssm_w_glu, proj_a, proj_b, proj_c, w_out))
    for l in range(depth):
        P, U = _inproj(x2d, norm_g[l].reshape(1, D), w_in_b, l, B, S)
        ya = _retention(P, cosf, sinf, dec, qd, kd, chunk_decay,
                        ret_q_norm[l], ret_k_norm[l] * (RET_QK_DIM ** -0.5),
                        ret_out_norm[l].reshape(1, -1), B, S)
        yb = _stick_breaking(P, jnp.tile(sb_q_norm[l], 2).reshape(1, -1),
                             jnp.tile(sb_k_norm[l], 2).reshape(1, -1), tri, B, S)
        are, aim, bre, bim, cre, cimn = _ssm_tables(ssm_a_re[l], ssm_a_im[l], ssm_log_dt[l], ssm_b_re[l],
                                                    ssm_b_im[l], ssm_c_re[l], ssm_c_im[l])
        yc = _ssm(U.reshape(S, B, COL_TILE), are, aim, bre, bim, cre, cimn,
                  ssm_d[l].reshape(1, -1), w_glu_b, l, ssm_b_glu[l].reshape(1, -1), B, S)
        x2d = _merge(ya, yb, yc.reshape(S, B * SSM_W), P, x2d, proj_a_b, proj_b_b, proj_c_b, w_out_b, l, B, S)
    return x2d.reshape(B, S, D)
```

```python
import functools
import math

import jax
import jax.numpy as jnp
from jax import lax
from jax.experimental import pallas as pl
from jax.experimental.pallas import tpu as pltpu

F32 = jnp.float32
BF16 = jnp.bfloat16

D_MODEL = 1024
RET_HEADS = 4
RET_QK_W = 512
RET_V_W = 1024
RET_QK_DIM = 128
RET_V_DIM = 256
RET_CHUNK = 128
RET_PREP_CHUNKS = 2
ROPE_BASE = 10000.0
SB_HEADS = 8
SB_W = 512
SB_HEAD_DIM = 64
SB_TILE = 256
SB_LANES = 128
SB_MASKED_LOGIT = -1e30
SB_MAX_LOGIT = 126.0
SB_UNROLL = 6
LOG2E = math.log2(math.e)
SSM_W = 512
SSM_GROUP = 16
SSM_GROUPS = 32
SSM_STATE = 64
SSM_STATES = SSM_GROUPS * SSM_STATE
EPS = 1e-6
IN_W = 9216

COL_TILE = 1024
INPROJ_ROWS = 2048
MERGE_ROWS = 512
SSM_COL_TILE = 5
P_W = IN_W - COL_TILE
P_RQ, P_RK = 0, 1
P_RV, P_RZ = 1, 2
P_SQ, P_SK, P_SV, P_SZ = 24, 28, 32, 36
P_GA, P_GB, P_GC = 5, 6, 7

SSM_T = 32
SSM_LANE_CHUNK = 512
VMEM_LIMIT = 56 * 1024 * 1024


def _sigmoid(v):
    return 1.0 / (1.0 + jnp.exp(-v))


def _nt_dot(a, b):
    return lax.dot_general(a, b, (((1,), (1,)), ((), ())), preferred_element_type=F32)


def _dot(a, b):
    return jnp.dot(a, b, preferred_element_type=F32)


def _inproj_kernel(x_ref, g_ref, w_ref, p_ref, u_ref, h_scr):
    j = pl.program_id(2)
    tm = x_ref.shape[0]
    rc = min(256, tm)

    @pl.when(j == 0)
    def _():
        def body(i, _):
            rows = pl.ds(pl.multiple_of(i * rc, rc), rc)
            x = x_ref[rows, :]
            ms = jnp.mean(x * x, axis=-1, keepdims=True)
            h_scr[rows, :] = (x * lax.rsqrt(ms + EPS) * g_ref[...]).astype(BF16)
            return 0
        lax.fori_loop(0, tm // rc, body, 0)

    mc = tm

    @pl.when(j != SSM_COL_TILE)
    def _():
        for r in range(0, tm, mc):
            p_ref[r:r + mc, :] = _dot(h_scr[r:r + mc, :], w_ref[...]).astype(BF16)

    @pl.when(j == SSM_COL_TILE)
    def _():
        for r in range(0, tm, mc):
            u_ref[r:r + mc, :] = _dot(h_scr[r:r + mc, :], w_ref[...]).astype(BF16)


def _inproj(x2d, g, w_all, layer, B, S):
    tm = min(INPROJ_ROWS, S)
    nt = S // tm
    n_col = IN_W // COL_TILE
    return pl.pallas_call(
        _inproj_kernel,
        out_shape=(jax.ShapeDtypeStruct((B * S, P_W), BF16),
                   jax.ShapeDtypeStruct((S, B * COL_TILE), BF16)),
        grid=(B, nt, n_col),
        in_specs=[
            pl.BlockSpec((tm, D_MODEL), lambda b, t, j: (b * nt + t, 0)),
            pl.BlockSpec((1, D_MODEL), lambda b, t, j: (0, 0)),
            pl.BlockSpec((None, D_MODEL, COL_TILE), lambda b, t, j: (layer, 0, j)),
        ],
        out_specs=(
            pl.BlockSpec((tm, COL_TILE),
                         lambda b, t, j: (b * nt + t, j - (j >= SSM_COL_TILE).astype(jnp.int32))),
            pl.BlockSpec((tm, COL_TILE), lambda b, t, j: (t, b)),
        ),
        scratch_shapes=[pltpu.VMEM((tm, D_MODEL), BF16)],
        compiler_params=pltpu.CompilerParams(
            dimension_semantics=("arbitrary", "arbitrary", "arbitrary"),
            vmem_limit_bytes=VMEM_LIMIT),
        name="inproj",
    )(x2d, g, w_all)


def _ret_kernel(chunk_decay, q_ref, k_ref, v_ref, z_ref, cos_ref, sin_ref, dec_ref, qd_ref, kd_ref,
                qg_ref, kg_ref, og_ref, perm_ref, avg_ref, eye_ref, o_ref,
                qr_scr, qdec_scr, kr_scr, kdt_scr, st_scr, stb_scr, sd_scr):
    S = q_ref.shape[0]
    n_chunks = S // RET_CHUNK
    heads = range(RET_HEADS)
    qk = [slice(h * RET_QK_DIM, (h + 1) * RET_QK_DIM) for h in heads]
    vv = [slice(h * RET_V_DIM, (h + 1) * RET_V_DIM) for h in heads]

    def chunk_rows(c):
        return pl.ds(pl.multiple_of(c * RET_CHUNK, RET_CHUNK), RET_CHUNK)

    def prep(i, _):
        cs = [i * RET_PREP_CHUNKS + u for u in range(min(RET_PREP_CHUNKS, n_chunks))]
        rows = [chunk_rows(c) for c in cs]
        raw = [[[ref[r, qk[h]] for h in heads] for ref in (q_ref, k_ref)] for r in rows]
        rolled = [[[_dot(t, perm_ref[...]) for t in ts] for ts in rw] for rw in raw]
        f32 = [[[t.astype(F32) for t in ts] for ts in rw] for rw in raw]
        mean_sq = [[[_dot((t * t).astype(BF16), avg_ref[...]) for t in ts] for ts in fw] for fw in f32]
        kdec = []
        for n, r in enumerate(rows):
            cosf = cos_ref[r, :]
            sinf = sin_ref[r, :]
            gains = ((qg_ref[0:1, :] * cosf, qg_ref[1:2, :] * sinf), (kg_ref[0:1, :] * cosf, kg_ref[1:2, :] * sinf))
            rot = [[(f32[n][a][h] * gains[a][0] + rolled[n][a][h] * gains[a][1])
                    * lax.rsqrt(mean_sq[n][a][h] + EPS) for h in heads] for a in range(2)]
            for h in heads:
                qr_scr[r, qk[h]] = rot[0][h].astype(BF16)
                qdec_scr[r, qk[h]] = (rot[0][h] * qd_ref[h]).astype(BF16)
                kr_scr[r, qk[h]] = rot[1][h].astype(BF16)
            kdec.append([(rot[1][h] * kd_ref[h]).astype(BF16) for h in heads])
        for n, c in enumerate(cs):
            for h in heads:
                kdt_scr[c, h] = _nt_dot(eye_ref[...], kdec[n][h]).astype(BF16)
        return 0

    lax.fori_loop(0, max(n_chunks // RET_PREP_CHUNKS, 1), prep, 0)

    st_scr[...] = jnp.zeros_like(st_scr)
    stb_scr[...] = jnp.zeros_like(stb_scr)

    def state_matmuls(c):
        rows = chunk_rows(c)
        scores = [_nt_dot(qr_scr[rows, qk[h]], kr_scr[rows, qk[h]]) for h in heads]
        kv = [_dot(kdt_scr[c, h], v_ref[rows, vv[h]]) for h in heads]
        return scores, kv

    def state_update(slot, scores, kv):
        for h in heads:
            sd_scr[slot, h] = (scores[h] * dec_ref[h]).astype(BF16)
            st = st_scr[h] * chunk_decay[h] + kv[h]
            st_scr[h] = st
            stb_scr[slot, h] = st.astype(BF16)

    def out_matmuls(c, slot):
        rows = chunk_rows(c)
        return [_dot(jnp.concatenate([sd_scr[slot, h], qdec_scr[rows, qk[h]]], axis=1),
                     jnp.concatenate([v_ref[rows, vv[h]], stb_scr[1 - slot, h]], axis=0)) for h in heads]

    def write_out(c, outs):
        rows = chunk_rows(c)
        for h in heads:
            o = outs[h]
            on = o * lax.rsqrt(jnp.mean(o * o, axis=-1, keepdims=True) + EPS) * og_ref[:, vv[h]]
            zz = z_ref[rows, vv[h]].astype(F32)
            o_ref[rows, vv[h]] = (on * (zz * _sigmoid(zz))).astype(BF16)

    state_update(0, *state_matmuls(0))

    def step(c, _):
        slot = c % 2
        sm = state_matmuls(c)
        outs = out_matmuls(c - 1, 1 - slot)
        state_update(slot, *sm)
        write_out(c - 1, outs)
        return 0

    lax.fori_loop(1, n_chunks, step, 0)
    last = n_chunks - 1
    write_out(last, out_matmuls(last, last % 2))


def _retention(P, cosf, sinf, dec, qd, kd, chunk_decay, qg, kg, og, B, S):
    half = RET_QK_DIM // 2
    lane = jnp.arange(RET_QK_DIM)
    perm = (lane[:, None] == (lane[None, :] + half) % RET_QK_DIM).astype(BF16)
    avg = jnp.full((RET_QK_DIM, RET_QK_DIM), 1.0 / RET_QK_DIM, BF16)
    eye = jnp.eye(RET_QK_DIM, dtype=BF16)
    with_rolled = lambda g: jnp.stack([g, jnp.roll(g, half)])
    const2 = lambda b: (0, 0)
    const3 = lambda b: (0, 0, 0)
    return pl.pallas_call(
        functools.partial(_ret_kernel, chunk_decay),
        out_shape=jax.ShapeDtypeStruct((B * S, RET_V_W), BF16),
        grid=(B,),
        in_specs=[
            pl.BlockSpec((S, RET_QK_W), lambda b: (b, P_RQ)),
            pl.BlockSpec((S, RET_QK_W), lambda b: (b, P_RK)),
            pl.BlockSpec((S, RET_V_W), lambda b: (b, P_RV)),
            pl.BlockSpec((S, RET_V_W), lambda b: (b, P_RZ)),
            pl.BlockSpec((S, RET_QK_DIM), const2),
            pl.BlockSpec((S, RET_QK_DIM), const2),
            pl.BlockSpec((RET_HEADS, RET_CHUNK, RET_CHUNK), const3),
            pl.BlockSpec((RET_HEADS, RET_CHUNK, RET_QK_DIM), const3),
            pl.BlockSpec((RET_HEADS, RET_CHUNK, RET_QK_DIM), const3),
            pl.BlockSpec((2, RET_QK_DIM), const2),
            pl.BlockSpec((2, RET_QK_DIM), const2),
            pl.BlockSpec((1, RET_V_W), const2),
            pl.BlockSpec((RET_QK_DIM, RET_QK_DIM), const2),
            pl.BlockSpec((RET_QK_DIM, RET_QK_DIM), const2),
            pl.BlockSpec((RET_QK_DIM, RET_QK_DIM), const2),
        ],
        out_specs=pl.BlockSpec((S, RET_V_W), lambda b: (b, 0)),
        scratch_shapes=[
            pltpu.VMEM((S, RET_QK_W), BF16),
            pltpu.VMEM((S, RET_QK_W), BF16),
            pltpu.VMEM((S, RET_QK_W), BF16),
            pltpu.VMEM((S // RET_CHUNK, RET_HEADS, RET_QK_DIM, RET_CHUNK), BF16),
            pltpu.VMEM((RET_HEADS, RET_QK_DIM, RET_V_DIM), F32),
            pltpu.VMEM((2, RET_HEADS, RET_QK_DIM, RET_V_DIM), BF16),
            pltpu.VMEM((2, RET_HEADS, RET_CHUNK, RET_CHUNK), BF16),
        ],
        compiler_params=pltpu.CompilerParams(
            dimension_semantics=("arbitrary",), vmem_limit_bytes=VMEM_LIMIT),
        name="retention",
    )(P, P, P, P, cosf, sinf, dec, qd, kd, with_rolled(qg), with_rolled(kg), og, perm, avg, eye)


def _sb_kernel(tab_ref, q_ref, k_ref, v_ref, z_ref, qg_ref, kg_ref, tri_ref, avg_ref, o_ref,
               qlo_scr, qhi_scr, kn_scr, vlo_scr, vhi_scr, nlk_scr, s_scr, rs_scr, w_scr, carry_scr, acc_scr):
    S = q_ref.shape[0]
    nc = min(512, S)
    lane_n = lax.broadcasted_iota(jnp.int32, (nc, 2 * SB_HEAD_DIM), 1)
    lo_n = lane_n < SB_HEAD_DIM

    def head_rms(ref, rows, g_ref):
        x = ref[rows, :].astype(F32)
        ms = _dot((x * x).astype(BF16), avg_ref[...])
        return x * lax.rsqrt(ms + EPS) * g_ref[...]

    def norm_body(i, _):
        rows = pl.ds(pl.multiple_of(i * nc, nc), nc)
        qn = head_rms(q_ref, rows, qg_ref) * (SB_HEAD_DIM ** -0.5 * LOG2E)
        qlo_scr[rows, :] = jnp.where(lo_n, qn, 0.0).astype(BF16)
        qhi_scr[rows, :] = jnp.where(lo_n, 0.0, qn).astype(BF16)
        kn_scr[rows, :] = head_rms(k_ref, rows, kg_ref).astype(BF16)
        v = v_ref[rows, :].astype(F32)
        vlo_scr[rows, :] = jnp.where(lo_n, v, 0.0).astype(BF16)
        vhi_scr[rows, :] = jnp.where(lo_n, 0.0, v).astype(BF16)
        return 0

    lax.fori_loop(0, S // nc, norm_body, 0)

    tile = nlk_scr.shape[2]
    n_tiles = S // tile
    n_sub = tile // SB_LANES
    n_diag = n_tiles
    n_off = n_tiles * (n_tiles - 1) // 2
    n_tab = n_diag + n_off
    qi_idx = lax.broadcasted_iota(jnp.int32, (tile, tile), 0)
    ki_idx = lax.broadcasted_iota(jnp.int32, (tile, tile), 1)
    causal = ki_idx < qi_idx

    def tile_rows(idx):
        return pl.ds(pl.multiple_of(idx * tile, tile), tile)

    def stage_scores(n, slot, masked):
        k2 = kn_scr[tile_rows(tab_ref[n_tab + n]), :]
        qrows = tile_rows(tab_ref[n])
        for hh, q_scr in enumerate((qlo_scr, qhi_scr)):
            z = jnp.minimum(_nt_dot(q_scr[qrows, :], k2), SB_MAX_LOGIT)
            nlk = jnp.log(1.0 + jnp.exp2(z)) * LOG2E
            s = z - nlk
            if masked:
                nlk = jnp.where(causal, nlk, 0.0)
                s = jnp.where(causal, s, SB_MASKED_LOGIT)
            nlk_scr[slot, hh] = nlk.astype(BF16)
            s_scr[slot, hh] = s
            rs_scr[slot, hh] = jnp.broadcast_to(jnp.sum(nlk, axis=-1, keepdims=True), (tile, SB_LANES))

    def stage_weights(n, slot, first):
        qi = tab_ref[n]
        for hh in range(2):
            carry = jnp.zeros((tile, SB_LANES), F32) if first else carry_scr[qi, hh]
            after = _dot(nlk_scr[slot, hh], tri_ref[...]) + jnp.concatenate([carry] * n_sub, axis=1)
            w = jnp.exp2(s_scr[slot, hh] + after)
            w_scr[slot, :, hh * tile:(hh + 1) * tile] = w.astype(BF16)
            carry_scr[qi, hh] = carry - rs_scr[slot, hh]

    def stage_values(n, slot, first):
        krows = tile_rows(tab_ref[n_tab + n])
        qi = tab_ref[n]
        vcat = jnp.concatenate([vlo_scr[krows, :], vhi_scr[krows, :]], axis=0)
        contrib = _dot(w_scr[slot], vcat)
        if first:
            acc_scr[qi] = contrib
        else:
            acc_scr[qi] += contrib

    def run_pipelined(n0, count, diag):
        def step(it, parity, a, b, c):
            if a:
                stage_scores(n0 + it, parity, diag)
            if b:
                stage_weights(n0 + it - 1, 1 - parity, diag)
            if c:
                stage_values(n0 + it - 2, parity, diag)

        def static_step(it):
            step(it, it % 2, 0 <= it < count, 0 <= it - 1 < count, 0 <= it - 2 < count)

        steady = list(range(2, count))
        for it in (0, 1):
            static_step(it)
        while len(steady) % SB_UNROLL:
            static_step(steady.pop(0))
        if steady:
            first = steady[0]

            def group(p, _):
                for u in range(SB_UNROLL):
                    step(first + SB_UNROLL * p + u, (first + u) % 2, True, True, True)
                return 0

            lax.fori_loop(0, len(steady) // SB_UNROLL, group, 0)
        for it in sorted({count, count + 1} - {0, 1}):
            static_step(it)

    run_pipelined(0, n_diag, True)
    run_pipelined(n_diag, n_off, False)

    def finish(qi, _):
        qrows = tile_rows(qi)
        zz = z_ref[qrows, :].astype(F32)
        o_ref[qrows, :] = (acc_scr[qi] * (zz * _sigmoid(zz))).astype(BF16)
        return 0

    lax.fori_loop(0, n_tiles, finish, 0)


def _sb_tile_table(n_tiles):
    tiles = [(i, i) for i in range(n_tiles)]
    tiles += [(i, j) for i in range(1, n_tiles) for j in range(i - 1, -1, -1)]
    return jnp.asarray([q for q, _ in tiles] + [k for _, k in tiles], dtype=jnp.int32)


def _stick_breaking(P, qg, kg, tri, B, S):
    n_pairs = SB_HEADS // 2
    pw = 2 * SB_HEAD_DIM
    tile = min(SB_TILE, S)
    n_tiles = S // tile
    head_of_lane = jnp.arange(pw) // SB_HEAD_DIM
    avg = (head_of_lane[:, None] == head_of_lane[None, :]).astype(BF16) * (1.0 / SB_HEAD_DIM)
    const2 = lambda b, p, tab: (0, 0)
    return pl.pallas_call(
        _sb_kernel,
        out_shape=jax.ShapeDtypeStruct((B * S, SB_W), BF16),
        grid_spec=pltpu.PrefetchScalarGridSpec(
            num_scalar_prefetch=1,
            grid=(B, n_pairs),
            in_specs=[
                pl.BlockSpec((S, pw), lambda b, p, tab: (b, P_SQ + p)),
                pl.BlockSpec((S, pw), lambda b, p, tab: (b, P_SK + p)),
                pl.BlockSpec((S, pw), lambda b, p, tab: (b, P_SV + p)),
                pl.BlockSpec((S, pw), lambda b, p, tab: (b, P_SZ + p)),
                pl.BlockSpec((1, pw), const2),
                pl.BlockSpec((1, pw), const2),
                pl.BlockSpec((tile, tile), const2),
                pl.BlockSpec((pw, pw), const2),
            ],
            out_specs=pl.BlockSpec((S, pw), lambda b, p, tab: (b, p)),
            scratch_shapes=[pltpu.VMEM((S, pw), BF16) for _ in range(5)] + [
                pltpu.VMEM((2, 2, tile, tile), BF16),
                pltpu.VMEM((2, 2, tile, tile), F32),
                pltpu.VMEM((2, 2, tile, SB_LANES), F32),
                pltpu.VMEM((2, tile, 2 * tile), BF16),
                pltpu.VMEM((n_tiles, 2, tile, SB_LANES), F32),
                pltpu.VMEM((n_tiles, tile, pw), F32),
            ]),
        compiler_params=pltpu.CompilerParams(
            dimension_semantics=("arbitrary", "arbitrary"), vmem_limit_bytes=VMEM_LIMIT),
        name="stickbreak",
    )(_sb_tile_table(n_tiles), P, P, P, P, qg, kg, tri, avg)


def _ssm_kernel(u_ref, uo_ref, are_ref, aim_ref, bre_ref, bim_ref, cre_ref, cimn_ref, d_ref, wg_ref, bg_ref,
                o_ref, bur0, bur1, bui0, bui1, hr0, hr1, hi0, hi1, cr_scr, ci_scr):
    g = pl.program_id(0)

    @pl.when(g == 0)
    def _():
        for ref in (cr_scr, ci_scr, bur1, bui1, hr0, hi0):
            ref[...] = jnp.zeros_like(ref)

    even = (bur0, bui0, hr0, hi0)
    odd = (bur1, bui1, hr1, hi1)
    refs = (u_ref, uo_ref, are_ref, aim_ref, bre_ref, bim_ref, cre_ref, cimn_ref, d_ref, wg_ref, bg_ref,
            o_ref, cr_scr, ci_scr)

    @pl.when(g % 2 == 0)
    def _():
        _ssm_step(refs, even, odd)

    @pl.when(g % 2 == 1)
    def _():
        _ssm_step(refs, odd, even)


def _ssm_step(refs, new, old):
    (u_ref, uo_ref, are_ref, aim_ref, bre_ref, bim_ref, cre_ref, cimn_ref, d_ref, wg_ref, bg_ref,
     o_ref, cr_scr, ci_scr) = refs
    bur_new, bui_new, hr_new, hi_new = new
    bur_old, bui_old, hr_old, hi_old = old
    T, B, _ = u_ref.shape
    rows_n = T * B
    half_k = SSM_W // 2
    half_n = SSM_STATES // 2

    def scan(piece):
        lc, half = divmod(piece, 2)
        ls = slice(lc * SSM_LANE_CHUNK, (lc + 1) * SSM_LANE_CHUNK)
        ar = jnp.broadcast_to(are_ref[:, ls], (B, SSM_LANE_CHUNK))
        ai = jnp.broadcast_to(aim_ref[:, ls], (B, SSM_LANE_CHUNK))
        hr = cr_scr[:, ls]
        hi = ci_scr[:, ls]
        for t in range(half * (T // 2), (half + 1) * (T // 2)):
            rows = slice(t * B, (t + 1) * B)
            hr, hi = (ar * hr - ai * hi + bur_old[rows, ls],
                      ar * hi + ai * hr + bui_old[rows, ls])
            hr_old[rows, ls] = hr.astype(BF16)
            hi_old[rows, ls] = hi.astype(BF16)
        cr_scr[:, ls] = hr
        ci_scr[:, ls] = hi

    pieces = list(range(2 * SSM_STATES // SSM_LANE_CHUNK))

    ys = []
    for c in range(2):
        ss = slice(c * half_n, (c + 1) * half_n)
        y_re = _dot(hr_new[:, ss], cre_ref[c])
        scan(pieces.pop(0))
        ys.append(y_re + _dot(hi_new[:, ss], cimn_ref[c]))
        scan(pieces.pop(0))

    u = u_ref[:, :, :SSM_W].reshape(rows_n, SSM_W)
    for c in range(2):
        uc = u[:, c * half_k:(c + 1) * half_k]
        bur_new[:, c * half_n:(c + 1) * half_n] = _dot(uc, bre_ref[c])
        scan(pieces.pop(0))
        bui_new[:, c * half_n:(c + 1) * half_n] = _dot(uc, bim_ref[c])
        scan(pieces.pop(0))
    assert not pieces

    uo = uo_ref[:, :, :SSM_W].reshape(rows_n, SSM_W).astype(F32)
    y = jnp.concatenate(ys, axis=1) + d_ref[...] * uo
    y = 0.5 * y * (1.0 + jnp.tanh(math.sqrt(2.0 / math.pi) * (y + 0.044715 * (y * y * y))))
    y = y * _sigmoid(_dot(y.astype(BF16), wg_ref[...]) + bg_ref[...])
    cz = uo_ref[:, :, SSM_W:].reshape(rows_n, SSM_W).astype(F32)
    o_ref[...] = (y * (cz * _sigmoid(cz))).astype(BF16).reshape(T, B, SSM_W)


def _ssm(U3, are, aim, bre, bim, cre, cimn, d, wg_all, layer, bg, B, S):
    T = min(SSM_T, S)
    rows_n = T * B
    n = S // T
    const2 = lambda g: (0, 0)
    const3 = lambda g: (0, 0, 0)
    cur = lambda g: (jnp.minimum(g, n - 1), 0, 0)
    done = lambda g: (jnp.clip(g - 2, 0, n - 1), 0, 0)
    return pl.pallas_call(
        _ssm_kernel,
        out_shape=jax.ShapeDtypeStruct((S, B, SSM_W), BF16),
        grid=(n + 2,),
        in_specs=[
            pl.BlockSpec((T, B, COL_TILE), cur),
            pl.BlockSpec((T, B, COL_TILE), done),
            pl.BlockSpec((1, SSM_STATES), const2),
            pl.BlockSpec((1, SSM_STATES), const2),
            pl.BlockSpec((2, SSM_W // 2, SSM_STATES // 2), const3),
            pl.BlockSpec((2, SSM_W // 2, SSM_STATES // 2), const3),
            pl.BlockSpec((2, SSM_STATES // 2, SSM_W // 2), const3),
            pl.BlockSpec((2, SSM_STATES // 2, SSM_W // 2), const3),
            pl.BlockSpec((1, SSM_W), const2),
            pl.BlockSpec((None, SSM_W, SSM_W), lambda g: (layer, 0, 0)),
            pl.BlockSpec((1, SSM_W), const2),
        ],
        out_specs=pl.BlockSpec((T, B, SSM_W), done),
        scratch_shapes=[
            *[pltpu.VMEM((rows_n, SSM_STATES), F32)] * 4,
            *[pltpu.VMEM((rows_n, SSM_STATES), BF16)] * 4,
            *[pltpu.VMEM((B, SSM_STATES), F32)] * 2,
        ],
        compiler_params=pltpu.CompilerParams(
            dimension_semantics=("arbitrary",), vmem_limit_bytes=VMEM_LIMIT),
        name="ssm",
    )(U3, U3, are, aim, bre, bim, cre, cimn, d, wg_all, bg)


def _merge_kernel(ya_ref, yb_ref, yc_ref, ga_ref, gb_ref, gc_ref, x_ref, pa_ref, pb_ref, pc_ref, wo_ref,
                  o_ref):
    tm = x_ref.shape[0]
    mc = min(MERGE_ROWS, tm)
    for r in range(0, tm, mc):
        rows = slice(r, r + mc)
        m = _sigmoid(ga_ref[rows, :].astype(F32)) * _dot(ya_ref[rows, :], pa_ref[...])
        m += _sigmoid(gb_ref[rows, :].astype(F32)) * _dot(yb_ref[rows, :], pb_ref[...])
        m += _sigmoid(gc_ref[rows, :].astype(F32)) * _dot(yc_ref[rows, :], pc_ref[...])
        o_ref[rows, :] = x_ref[rows, :] + _dot(m.astype(BF16), wo_ref[...])


def _merge(ya, yb, yc2d, P, x2d, pa_all, pb_all, pc_all, wo_all, layer, B, S):
    tm = min(1024, S)
    nt = S // tm
    tok = lambda b, t: (b * nt + t, 0)
    of_layer = lambda b, t: (layer, 0, 0)
    return pl.pallas_call(
        _merge_kernel,
        out_shape=jax.ShapeDtypeStruct((B * S, D_MODEL), F32),
        grid=(B, nt),
        in_specs=[
            pl.BlockSpec((tm, RET_V_W), tok),
            pl.BlockSpec((tm, SB_W), tok),
            pl.BlockSpec((tm, SSM_W), lambda b, t: (t, b)),
            pl.BlockSpec((tm, D_MODEL), lambda b, t: (b * nt + t, P_GA)),
            pl.BlockSpec((tm, D_MODEL), lambda b, t: (b * nt + t, P_GB)),
            pl.BlockSpec((tm, D_MODEL), lambda b, t: (b * nt + t, P_GC)),
            pl.BlockSpec((tm, D_MODEL), tok),
            pl.BlockSpec((None, RET_V_W, D_MODEL), of_layer),
            pl.BlockSpec((None, SB_W, D_MODEL), of_layer),
            pl.BlockSpec((None, SSM_W, D_MODEL), of_layer),
            pl.BlockSpec((None, D_MODEL, D_MODEL), of_layer),
        ],
        out_specs=pl.BlockSpec((tm, D_MODEL), tok),
        compiler_params=pltpu.CompilerParams(
            dimension_semantics=("arbitrary", "arbitrary"), vmem_limit_bytes=VMEM_LIMIT),
        name="merge",
    )(ya, yb, yc2d, P, P, P, x2d, pa_all, pb_all, pc_all, wo_all)


def _rope_tables(S):
    half = RET_QK_DIM // 2
    inv_freq = ROPE_BASE ** (-jnp.arange(half, dtype=F32) / half)
    ang = jnp.arange(S, dtype=F32)[:, None] * inv_freq[None, :]
    cos, sin = jnp.cos(ang), jnp.sin(ang)
    return jnp.concatenate([cos, cos], axis=1), jnp.concatenate([-sin, sin], axis=1)


def _decay_tables():
    log_gamma = jnp.log1p(-jnp.exp2(-5.0 - jnp.arange(RET_HEADS, dtype=F32)))
    idx = jnp.arange(RET_CHUNK, dtype=F32)
    rel = idx[:, None] - idx[None, :]
    dec = jnp.where(rel >= 0, jnp.exp(jnp.maximum(rel, 0.0)[None] * log_gamma[:, None, None]), 0.0)
    kdec = jnp.exp((RET_CHUNK - 1 - idx)[None, :] * log_gamma[:, None])
    qdec = jnp.exp((idx + 1)[None, :] * log_gamma[:, None])
    full = lambda a: jnp.broadcast_to(a[:, :, None], (RET_HEADS, RET_CHUNK, RET_QK_DIM))
    chunk_decay = tuple(float(math.exp(RET_CHUNK * math.log1p(-2.0 ** (-5.0 - h)))) for h in range(RET_HEADS))
    return dec, full(qdec), full(kdec), chunk_decay


def _ssm_tables(a_re, a_im, log_dt, b_re, b_im, c_re, c_im):
    dt = jnp.exp(log_dt)[:, None]
    mag = jnp.exp(dt * a_re)
    ab_re = mag * jnp.cos(dt * a_im)
    ab_im = mag * jnp.sin(dt * a_im)
    den = a_re * a_re + a_im * a_im
    nr = ab_re - 1.0
    coef_re = (nr * a_re + ab_im * a_im) / den
    coef_im = (ab_im * a_re - nr * a_im) / den
    bb_re = coef_re[..., None] * b_re - coef_im[..., None] * b_im
    bb_im = coef_re[..., None] * b_im + coef_im[..., None] * b_re
    gl = SSM_GROUPS // 2
    eye = jnp.eye(gl, dtype=F32)

    def b_blocks(bb):
        return jnp.einsum('cgpm,gh->cgmhp', bb.reshape(2, gl, SSM_STATE, SSM_GROUP), eye).reshape(
            2, gl * SSM_GROUP, gl * SSM_STATE).astype(BF16)

    def c_blocks(cc):
        return jnp.einsum('cgmp,gh->cgphm', cc.reshape(2, gl, SSM_GROUP, SSM_STATE), eye).reshape(
            2, gl * SSM_STATE, gl * SSM_GROUP).astype(BF16)

    return (ab_re.reshape(1, SSM_STATES), ab_im.reshape(1, SSM_STATES),
            b_blocks(bb_re), b_blocks(bb_im), c_blocks(c_re), c_blocks(-c_im))


def _suffix_sum_matrix(tile):
    k = jnp.arange(tile)
    return -(k[:, None] > k[None, :]).astype(BF16)


def kernel(x, norm_g, w_in, ret_q_norm, ret_k_norm, ret_out_norm, sb_q_norm, sb_k_norm, ssm_a_re, ssm_a_im,
           ssm_log_dt, ssm_b_re, ssm_b_im, ssm_c_re, ssm_c_im, ssm_d, ssm_w_glu, ssm_b_glu, proj_a, proj_b,
           proj_c, w_out):
    B, S, D = x.shape
    depth = w_in.shape[0]
    assert D == D_MODEL and B == 16 and S % 128 == 0
    cosf, sinf = _rope_tables(S)
    dec, qd, kd, chunk_decay = _decay_tables()
    tri = _suffix_sum_matrix(min(SB_TILE, S))
    x2d = x.reshape(B * S, D)
    w_in_b, w_glu_b, proj_a_b, proj_b_b, proj_c_b, w_out_b = (
        w.astype(BF16) for w in (w_in, ssm_w_glu, proj_a, proj_b, proj_c, w_out))
    for l in range(depth):
        P, U = _inproj(x2d, norm_g[l].reshape(1, D), w_in_b, l, B, S)
        ya = _retention(P, cosf, sinf, dec, qd, kd, chunk_decay,
                        ret_q_norm[l], ret_k_norm[l] * (RET_QK_DIM ** -0.5),
                        ret_out_norm[l].reshape(1, -1), B, S)
        yb = _stick_breaking(P, jnp.tile(sb_q_norm[l], 2).reshape(1, -1),
                             jnp.tile(sb_k_norm[l], 2).reshape(1, -1), tri, B, S)
        are, aim, bre, bim, cre, cimn = _ssm_tables(ssm_a_re[l], ssm_a_im[l], ssm_log_dt[l], ssm_b_re[l],
                                                    ssm_b_im[l], ssm_c_re[l], ssm_c_im[l])
        yc = _ssm(U.reshape(S, B, COL_TILE), are, aim, bre, bim, cre, cimn,
                  ssm_d[l].reshape(1, -1), w_glu_b, l, ssm_b_glu[l].reshape(1, -1), B, S)
        x2d = _merge(ya, yb, yc.reshape(S, B * SSM_W), P, x2d, proj_a_b, proj_b_b, proj_c_b, w_out_b, l, B, S)
    return x2d.reshape(B, S, D)
```

```python
import functools
import math

import jax
import jax.numpy as jnp
from jax import lax
from jax.experimental import pallas as pl
from jax.experimental.pallas import tpu as pltpu

F32 = jnp.float32
BF16 = jnp.bfloat16

D_MODEL = 1024
RET_HEADS = 4
RET_QK_W = 512
RET_V_W = 1024
RET_QK_DIM = 128
RET_V_DIM = 256
RET_CHUNK = 128
RET_PREP_CHUNKS = 4
ROPE_BASE = 10000.0
SB_HEADS = 8
SB_W = 512
SB_HEAD_DIM = 64
SB_TILE = 256
SB_LANES = 128
SB_MASKED_LOGIT = -1e30
SB_MAX_LOGIT = 126.0
SB_UNROLL = 8
LOG2E = math.log2(math.e)
SSM_W = 512
SSM_GROUP = 16
SSM_GROUPS = 32
SSM_STATE = 64
SSM_STATES = SSM_GROUPS * SSM_STATE
EPS = 1e-6
IN_W = 9216

COL_TILE = 1024
INPROJ_ROWS = 2048
MERGE_ROWS = 512
SSM_COL_TILE = 5
P_W = IN_W - COL_TILE
P_RQ, P_RK = 0, 1
P_RV, P_RZ = 1, 2
P_SQ, P_SK, P_SV, P_SZ = 24, 28, 32, 36
P_GA, P_GB, P_GC = 5, 6, 7

SSM_T = 32
SSM_LANE_CHUNK = 512
VMEM_LIMIT = 56 * 1024 * 1024


def _sigmoid(v):
    return 1.0 / (1.0 + jnp.exp(-v))


def _nt_dot(a, b):
    return lax.dot_general(a, b, (((1,), (1,)), ((), ())), preferred_element_type=F32)


def _dot(a, b):
    return jnp.dot(a, b, preferred_element_type=F32)


def _inproj_kernel(x_ref, g_ref, w_ref, p_ref, u_ref, h_scr):
    j = pl.program_id(2)
    tm = x_ref.shape[0]
    rc = min(256, tm)

    @pl.when(j == 0)
    def _():
        def body(i, _):
            rows = pl.ds(pl.multiple_of(i * rc, rc), rc)
            x = x_ref[rows, :]
            ms = jnp.mean(x * x, axis=-1, keepdims=True)
            h_scr[rows, :] = (x * lax.rsqrt(ms + EPS) * g_ref[...]).astype(BF16)
            return 0
        lax.fori_loop(0, tm // rc, body, 0)

    mc = tm

    @pl.when(j != SSM_COL_TILE)
    def _():
        for r in range(0, tm, mc):
            p_ref[r:r + mc, :] = _dot(h_scr[r:r + mc, :], w_ref[...]).astype(BF16)

    @pl.when(j == SSM_COL_TILE)
    def _():
        for r in range(0, tm, mc):
            u_ref[r:r + mc, :] = _dot(h_scr[r:r + mc, :], w_ref[...]).astype(BF16)


def _inproj(x2d, g, w_all, layer, B, S):
    tm = min(INPROJ_ROWS, S)
    nt = S // tm
    n_col = IN_W // COL_TILE
    return pl.pallas_call(
        _inproj_kernel,
        out_shape=(jax.ShapeDtypeStruct((B * S, P_W), BF16),
                   jax.ShapeDtypeStruct((S, B * COL_TILE), BF16)),
        grid=(B, nt, n_col),
        in_specs=[
            pl.BlockSpec((tm, D_MODEL), lambda b, t, j: (b * nt + t, 0)),
            pl.BlockSpec((1, D_MODEL), lambda b, t, j: (0, 0)),
            pl.BlockSpec((None, D_MODEL, COL_TILE), lambda b, t, j: (layer, 0, j)),
        ],
        out_specs=(
            pl.BlockSpec((tm, COL_TILE),
                         lambda b, t, j: (b * nt + t, j - (j >= SSM_COL_TILE).astype(jnp.int32))),
            pl.BlockSpec((tm, COL_TILE), lambda b, t, j: (t, b)),
        ),
        scratch_shapes=[pltpu.VMEM((tm, D_MODEL), BF16)],
        compiler_params=pltpu.CompilerParams(
            dimension_semantics=("arbitrary", "arbitrary", "arbitrary"),
            vmem_limit_bytes=VMEM_LIMIT),
        name="inproj",
    )(x2d, g, w_all)


def _ret_kernel(chunk_decay, q_ref, k_ref, v_ref, z_ref, cos_ref, sin_ref, dec_ref, qd_ref, kd_ref,
                qg_ref, kg_ref, og_ref, perm_ref, avg_ref, eye_ref, o_ref,
                qr_scr, qdec_scr, kr_scr, kdt_scr, st_scr, stb_scr, sd_scr):
    S = q_ref.shape[0]
    n_chunks = S // RET_CHUNK
    heads = range(RET_HEADS)
    qk = [slice(h * RET_QK_DIM, (h + 1) * RET_QK_DIM) for h in heads]
    vv = [slice(h * RET_V_DIM, (h + 1) * RET_V_DIM) for h in heads]

    def chunk_rows(c):
        return pl.ds(pl.multiple_of(c * RET_CHUNK, RET_CHUNK), RET_CHUNK)

    def prep(i, _):
        cs = [i * RET_PREP_CHUNKS + u for u in range(min(RET_PREP_CHUNKS, n_chunks))]
        rows = [chunk_rows(c) for c in cs]
        raw = [[[ref[r, qk[h]] for h in heads] for ref in (q_ref, k_ref)] for r in rows]
        rolled = [[[_dot(t, perm_ref[...]) for t in ts] for ts in rw] for rw in raw]
        f32 = [[[t.astype(F32) for t in ts] for ts in rw] for rw in raw]
        mean_sq = [[[_dot((t * t).astype(BF16), avg_ref[...]) for t in ts] for ts in fw] for fw in f32]
        kdec = []
        for n, r in enumerate(rows):
            cosf = cos_ref[r, :]
            sinf = sin_ref[r, :]
            gains = ((qg_ref[0:1, :] * cosf, qg_ref[1:2, :] * sinf), (kg_ref[0:1, :] * cosf, kg_ref[1:2, :] * sinf))
            rot = [[(f32[n][a][h] * gains[a][0] + rolled[n][a][h] * gains[a][1])
                    * lax.rsqrt(mean_sq[n][a][h] + EPS) for h in heads] for a in range(2)]
            for h in heads:
                qr_scr[r, qk[h]] = rot[0][h].astype(BF16)
                qdec_scr[r, qk[h]] = (rot[0][h] * qd_ref[h]).astype(BF16)
                kr_scr[r, qk[h]] = rot[1][h].astype(BF16)
            kdec.append([(rot[1][h] * kd_ref[h]).astype(BF16) for h in heads])
        for n, c in enumerate(cs):
            for h in heads:
                kdt_scr[c, h] = _nt_dot(eye_ref[...], kdec[n][h]).astype(BF16)
        return 0

    lax.fori_loop(0, max(n_chunks // RET_PREP_CHUNKS, 1), prep, 0)

    st_scr[...] = jnp.zeros_like(st_scr)
    stb_scr[...] = jnp.zeros_like(stb_scr)

    def state_matmuls(c):
        rows = chunk_rows(c)
        scores = [_nt_dot(qr_scr[rows, qk[h]], kr_scr[rows, qk[h]]) for h in heads]
        kv = [_dot(kdt_scr[c, h], v_ref[rows, vv[h]]) for h in heads]
        return scores, kv

    def state_update(slot, scores, kv):
        for h in heads:
            sd_scr[slot, h] = (scores[h] * dec_ref[h]).astype(BF16)
            st = st_scr[h] * chunk_decay[h] + kv[h]
            st_scr[h] = st
            stb_scr[slot, h] = st.astype(BF16)

    def out_matmuls(c, slot):
        rows = chunk_rows(c)
        return [_dot(jnp.concatenate([sd_scr[slot, h], qdec_scr[rows, qk[h]]], axis=1),
                     jnp.concatenate([v_ref[rows, vv[h]], stb_scr[1 - slot, h]], axis=0)) for h in heads]

    def write_out(c, outs):
        rows = chunk_rows(c)
        for h in heads:
            o = outs[h]
            on = o * lax.rsqrt(jnp.mean(o * o, axis=-1, keepdims=True) + EPS) * og_ref[:, vv[h]]
            zz = z_ref[rows, vv[h]].astype(F32)
            o_ref[rows, vv[h]] = (on * (zz * _sigmoid(zz))).astype(BF16)

    state_update(0, *state_matmuls(0))

    def step(c, _):
        slot = c % 2
        sm = state_matmuls(c)
        outs = out_matmuls(c - 1, 1 - slot)
        state_update(slot, *sm)
        write_out(c - 1, outs)
        return 0

    lax.fori_loop(1, n_chunks, step, 0)
    last = n_chunks - 1
    write_out(last, out_matmuls(last, last % 2))


def _retention(P, cosf, sinf, dec, qd, kd, chunk_decay, qg, kg, og, B, S):
    half = RET_QK_DIM // 2
    lane = jnp.arange(RET_QK_DIM)
    perm = (lane[:, None] == (lane[None, :] + half) % RET_QK_DIM).astype(BF16)
    avg = jnp.full((RET_QK_DIM, RET_QK_DIM), 1.0 / RET_QK_DIM, BF16)
    eye = jnp.eye(RET_QK_DIM, dtype=BF16)
    with_rolled = lambda g: jnp.stack([g, jnp.roll(g, half)])
    const2 = lambda b: (0, 0)
    const3 = lambda b: (0, 0, 0)
    return pl.pallas_call(
        functools.partial(_ret_kernel, chunk_decay),
        out_shape=jax.ShapeDtypeStruct((B * S, RET_V_W), BF16),
        grid=(B,),
        in_specs=[
            pl.BlockSpec((S, RET_QK_W), lambda b: (b, P_RQ)),
            pl.BlockSpec((S, RET_QK_W), lambda b: (b, P_RK)),
            pl.BlockSpec((S, RET_V_W), lambda b: (b, P_RV)),
            pl.BlockSpec((S, RET_V_W), lambda b: (b, P_RZ)),
            pl.BlockSpec((S, RET_QK_DIM), const2),
            pl.BlockSpec((S, RET_QK_DIM), const2),
            pl.BlockSpec((RET_HEADS, RET_CHUNK, RET_CHUNK), const3),
            pl.BlockSpec((RET_HEADS, RET_CHUNK, RET_QK_DIM), const3),
            pl.BlockSpec((RET_HEADS, RET_CHUNK, RET_QK_DIM), const3),
            pl.BlockSpec((2, RET_QK_DIM), const2),
            pl.BlockSpec((2, RET_QK_DIM), const2),
            pl.BlockSpec((1, RET_V_W), const2),
            pl.BlockSpec((RET_QK_DIM, RET_QK_DIM), const2),
            pl.BlockSpec((RET_QK_DIM, RET_QK_DIM), const2),
            pl.BlockSpec((RET_QK_DIM, RET_QK_DIM), const2),
        ],
        out_specs=pl.BlockSpec((S, RET_V_W), lambda b: (b, 0)),
        scratch_shapes=[
            pltpu.VMEM((S, RET_QK_W), BF16),
            pltpu.VMEM((S, RET_QK_W), BF16),
            pltpu.VMEM((S, RET_QK_W), BF16),
            pltpu.VMEM((S // RET_CHUNK, RET_HEADS, RET_QK_DIM, RET_CHUNK), BF16),
            pltpu.VMEM((RET_HEADS, RET_QK_DIM, RET_V_DIM), F32),
            pltpu.VMEM((2, RET_HEADS, RET_QK_DIM, RET_V_DIM), BF16),
            pltpu.VMEM((2, RET_HEADS, RET_CHUNK, RET_CHUNK), BF16),
        ],
        compiler_params=pltpu.CompilerParams(
            dimension_semantics=("arbitrary",), vmem_limit_bytes=VMEM_LIMIT),
        name="retention",
    )(P, P, P, P, cosf, sinf, dec, qd, kd, with_rolled(qg), with_rolled(kg), og, perm, avg, eye)


def _sb_kernel(tab_ref, q_ref, k_ref, v_ref, z_ref, qg_ref, kg_ref, tri_ref, avg_ref, o_ref,
               qlo_scr, qhi_scr, kn_scr, vlo_scr, vhi_scr, nlk_scr, s_scr, rs_scr, w_scr, carry_scr, acc_scr):
    S = q_ref.shape[0]
    nc = min(512, S)
    lane_n = lax.broadcasted_iota(jnp.int32, (nc, 2 * SB_HEAD_DIM), 1)
    lo_n = lane_n < SB_HEAD_DIM

    def head_rms(ref, rows, g_ref):
        x = ref[rows, :].astype(F32)
        ms = _dot((x * x).astype(BF16), avg_ref[...])
        return x * lax.rsqrt(ms + EPS) * g_ref[...]

    def norm_body(i, _):
        rows = pl.ds(pl.multiple_of(i * nc, nc), nc)
        qn = head_rms(q_ref, rows, qg_ref) * (SB_HEAD_DIM ** -0.5 * LOG2E)
        qlo_scr[rows, :] = jnp.where(lo_n, qn, 0.0).astype(BF16)
        qhi_scr[rows, :] = jnp.where(lo_n, 0.0, qn).astype(BF16)
        kn_scr[rows, :] = head_rms(k_ref, rows, kg_ref).astype(BF16)
        v = v_ref[rows, :].astype(F32)
        vlo_scr[rows, :] = jnp.where(lo_n, v, 0.0).astype(BF16)
        vhi_scr[rows, :] = jnp.where(lo_n, 0.0, v).astype(BF16)
        return 0

    lax.fori_loop(0, S // nc, norm_body, 0)

    tile = nlk_scr.shape[2]
    n_tiles = S // tile
    n_sub = tile // SB_LANES
    n_diag = n_tiles
    n_off = n_tiles * (n_tiles - 1) // 2
    n_tab = n_diag + n_off
    qi_idx = lax.broadcasted_iota(jnp.int32, (tile, tile), 0)
    ki_idx = lax.broadcasted_iota(jnp.int32, (tile, tile), 1)
    causal = ki_idx < qi_idx

    def tile_rows(idx):
        return pl.ds(pl.multiple_of(idx * tile, tile), tile)

    def stage_scores(n, slot, masked):
        k2 = kn_scr[tile_rows(tab_ref[n_tab + n]), :]
        qrows = tile_rows(tab_ref[n])
        for hh, q_scr in enumerate((qlo_scr, qhi_scr)):
            z = jnp.minimum(_nt_dot(q_scr[qrows, :], k2), SB_MAX_LOGIT)
            nlk = jnp.log(1.0 + jnp.exp2(z)) * LOG2E
            s = z - nlk
            if masked:
                nlk = jnp.where(causal, nlk, 0.0)
                s = jnp.where(causal, s, SB_MASKED_LOGIT)
            nlk_scr[slot, hh] = nlk.astype(BF16)
            s_scr[slot, hh] = s
            rs_scr[slot, hh] = jnp.broadcast_to(jnp.sum(nlk, axis=-1, keepdims=True), (tile, SB_LANES))

    def stage_weights(n, slot, first):
        qi = tab_ref[n]
        for hh in range(2):
            carry = jnp.zeros((tile, SB_LANES), F32) if first else carry_scr[qi, hh]
            after = _dot(nlk_scr[slot, hh], tri_ref[...]) + jnp.concatenate([carry] * n_sub, axis=1)
            w = jnp.exp2(s_scr[slot, hh] + after)
            w_scr[slot, :, hh * tile:(hh + 1) * tile] = w.astype(BF16)
            carry_scr[qi, hh] = carry - rs_scr[slot, hh]

    def stage_values(n, slot, first):
        krows = tile_rows(tab_ref[n_tab + n])
        qi = tab_ref[n]
        vcat = jnp.concatenate([vlo_scr[krows, :], vhi_scr[krows, :]], axis=0)
        contrib = _dot(w_scr[slot], vcat)
        if first:
            acc_scr[qi] = contrib
        else:
            acc_scr[qi] += contrib

    def run_pipelined(n0, count, diag):
        def step(it, parity, a, b, c):
            if a:
                stage_scores(n0 + it, parity, diag)
            if b:
                stage_weights(n0 + it - 1, 1 - parity, diag)
            if c:
                stage_values(n0 + it - 2, parity, diag)

        def static_step(it):
            step(it, it % 2, 0 <= it < count, 0 <= it - 1 < count, 0 <= it - 2 < count)

        steady = list(range(2, count))
        for it in (0, 1):
            static_step(it)
        while len(steady) % SB_UNROLL:
            static_step(steady.pop(0))
        if steady:
            first = steady[0]

            def group(p, _):
                for u in range(SB_UNROLL):
                    step(first + SB_UNROLL * p + u, (first + u) % 2, True, True, True)
                return 0

            lax.fori_loop(0, len(steady) // SB_UNROLL, group, 0)
        for it in sorted({count, count + 1} - {0, 1}):
            static_step(it)

    run_pipelined(0, n_diag, True)
    run_pipelined(n_diag, n_off, False)

    def finish(qi, _):
        qrows = tile_rows(qi)
        zz = z_ref[qrows, :].astype(F32)
        o_ref[qrows, :] = (acc_scr[qi] * (zz * _sigmoid(zz))).astype(BF16)
        return 0

    lax.fori_loop(0, n_tiles, finish, 0)


def _sb_tile_table(n_tiles):
    tiles = [(i, i) for i in range(n_tiles)]
    tiles += [(i, j) for i in range(1, n_tiles) for j in range(i - 1, -1, -1)]
    return jnp.asarray([q for q, _ in tiles] + [k for _, k in tiles], dtype=jnp.int32)


def _stick_breaking(P, qg, kg, tri, B, S):
    n_pairs = SB_HEADS // 2
    pw = 2 * SB_HEAD_DIM
    tile = min(SB_TILE, S)
    n_tiles = S // tile
    head_of_lane = jnp.arange(pw) // SB_HEAD_DIM
    avg = (head_of_lane[:, None] == head_of_lane[None, :]).astype(BF16) * (1.0 / SB_HEAD_DIM)
    const2 = lambda b, p, tab: (0, 0)
    return pl.pallas_call(
        _sb_kernel,
        out_shape=jax.ShapeDtypeStruct((B * S, SB_W), BF16),
        grid_spec=pltpu.PrefetchScalarGridSpec(
            num_scalar_prefetch=1,
            grid=(B, n_pairs),
            in_specs=[
                pl.BlockSpec((S, pw), lambda b, p, tab: (b, P_SQ + p)),
                pl.BlockSpec((S, pw), lambda b, p, tab: (b, P_SK + p)),
                pl.BlockSpec((S, pw), lambda b, p, tab: (b, P_SV + p)),
                pl.BlockSpec((S, pw), lambda b, p, tab: (b, P_SZ + p)),
                pl.BlockSpec((1, pw), const2),
                pl.BlockSpec((1, pw), const2),
                pl.BlockSpec((tile, tile), const2),
                pl.BlockSpec((pw, pw), const2),
            ],
            out_specs=pl.BlockSpec((S, pw), lambda b, p, tab: (b, p)),
            scratch_shapes=[pltpu.VMEM((S, pw), BF16) for _ in range(5)] + [
                pltpu.VMEM((2, 2, tile, tile), BF16),
                pltpu.VMEM((2, 2, tile, tile), F32),
                pltpu.VMEM((2, 2, tile, SB_LANES), F32),
                pltpu.VMEM((2, tile, 2 * tile), BF16),
                pltpu.VMEM((n_tiles, 2, tile, SB_LANES), F32),
                pltpu.VMEM((n_tiles, tile, pw), F32),
            ]),
        compiler_params=pltpu.CompilerParams(
            dimension_semantics=("arbitrary", "arbitrary"), vmem_limit_bytes=VMEM_LIMIT),
        name="stickbreak",
    )(_sb_tile_table(n_tiles), P, P, P, P, qg, kg, tri, avg)


def _ssm_kernel(u_ref, uo_ref, are_ref, aim_ref, bre_ref, bim_ref, cre_ref, cimn_ref, d_ref, wg_ref, bg_ref,
                o_ref, bur0, bur1, bui0, bui1, hr0, hr1, hi0, hi1, cr_scr, ci_scr):
    g = pl.program_id(0)

    @pl.when(g == 0)
    def _():
        for ref in (cr_scr, ci_scr, bur1, bui1, hr0, hi0):
            ref[...] = jnp.zeros_like(ref)

    even = (bur0, bui0, hr0, hi0)
    odd = (bur1, bui1, hr1, hi1)
    refs = (u_ref, uo_ref, are_ref, aim_ref, bre_ref, bim_ref, cre_ref, cimn_ref, d_ref, wg_ref, bg_ref,
            o_ref, cr_scr, ci_scr)

    @pl.when(g % 2 == 0)
    def _():
        _ssm_step(refs, even, odd)

    @pl.when(g % 2 == 1)
    def _():
        _ssm_step(refs, odd, even)


def _ssm_step(refs, new, old):
    (u_ref, uo_ref, are_ref, aim_ref, bre_ref, bim_ref, cre_ref, cimn_ref, d_ref, wg_ref, bg_ref,
     o_ref, cr_scr, ci_scr) = refs
    bur_new, bui_new, hr_new, hi_new = new
    bur_old, bui_old, hr_old, hi_old = old
    T, B, _ = u_ref.shape
    rows_n = T * B
    half_k = SSM_W // 2
    half_n = SSM_STATES // 2

    def scan(piece):
        lc, half = divmod(piece, 2)
        ls = slice(lc * SSM_LANE_CHUNK, (lc + 1) * SSM_LANE_CHUNK)
        ar = jnp.broadcast_to(are_ref[:, ls], (B, SSM_LANE_CHUNK))
        ai = jnp.broadcast_to(aim_ref[:, ls], (B, SSM_LANE_CHUNK))
        hr = cr_scr[:, ls]
        hi = ci_scr[:, ls]
        for t in range(half * (T // 2), (half + 1) * (T // 2)):
            rows = slice(t * B, (t + 1) * B)
            hr, hi = (ar * hr - ai * hi + bur_old[rows, ls],
                      ar * hi + ai * hr + bui_old[rows, ls])
            hr_old[rows, ls] = hr.astype(BF16)
            hi_old[rows, ls] = hi.astype(BF16)
        cr_scr[:, ls] = hr
        ci_scr[:, ls] = hi

    pieces = list(range(2 * SSM_STATES // SSM_LANE_CHUNK))

    ys = []
    for c in range(2):
        ss = slice(c * half_n, (c + 1) * half_n)
        y_re = _dot(hr_new[:, ss], cre_ref[c])
        scan(pieces.pop(0))
        ys.append(y_re + _dot(hi_new[:, ss], cimn_ref[c]))
        scan(pieces.pop(0))

    u = u_ref[:, :, :SSM_W].reshape(rows_n, SSM_W)
    for c in range(2):
        uc = u[:, c * half_k:(c + 1) * half_k]
        bur_new[:, c * half_n:(c + 1) * half_n] = _dot(uc, bre_ref[c])
        scan(pieces.pop(0))
        bui_new[:, c * half_n:(c + 1) * half_n] = _dot(uc, bim_ref[c])
        scan(pieces.pop(0))
    assert not pieces

    uo = uo_ref[:, :, :SSM_W].reshape(rows_n, SSM_W).astype(F32)
    y = jnp.concatenate(ys, axis=1) + d_ref[...] * uo
    y = 0.5 * y * (1.0 + jnp.tanh(math.sqrt(2.0 / math.pi) * (y + 0.044715 * (y * y * y))))
    y = y * _sigmoid(_dot(y.astype(BF16), wg_ref[...]) + bg_ref[...])
    cz = uo_ref[:, :, SSM_W:].reshape(rows_n, SSM_W).astype(F32)
    o_ref[...] = (y * (cz * _sigmoid(cz))).astype(BF16).reshape(T, B, SSM_W)


def _ssm(U3, are, aim, bre, bim, cre, cimn, d, wg_all, layer, bg, B, S):
    T = min(SSM_T, S)
    rows_n = T * B
    n = S // T
    const2 = lambda g: (0, 0)
    const3 = lambda g: (0, 0, 0)
    cur = lambda g: (jnp.minimum(g, n - 1), 0, 0)
    done = lambda g: (jnp.clip(g - 2, 0, n - 1), 0, 0)
    return pl.pallas_call(
        _ssm_kernel,
        out_shape=jax.ShapeDtypeStruct((S, B, SSM_W), BF16),
        grid=(n + 2,),
        in_specs=[
            pl.BlockSpec((T, B, COL_TILE), cur),
            pl.BlockSpec((T, B, COL_TILE), done),
            pl.BlockSpec((1, SSM_STATES), const2),
            pl.BlockSpec((1, SSM_STATES), const2),
            pl.BlockSpec((2, SSM_W // 2, SSM_STATES // 2), const3),
            pl.BlockSpec((2, SSM_W // 2, SSM_STATES // 2), const3),
            pl.BlockSpec((2, SSM_STATES // 2, SSM_W // 2), const3),
            pl.BlockSpec((2, SSM_STATES // 2, SSM_W // 2), const3),
            pl.BlockSpec((1, SSM_W), const2),
            pl.BlockSpec((None, SSM_W, SSM_W), lambda g: (layer, 0, 0)),
            pl.BlockSpec((1, SSM_W), const2),
        ],
        out_specs=pl.BlockSpec((T, B, SSM_W), done),
        scratch_shapes=[
            *[pltpu.VMEM((rows_n, SSM_STATES), F32)] * 4,
            *[pltpu.VMEM((rows_n, SSM_STATES), BF16)] * 4,
            *[pltpu.VMEM((B, SSM_STATES), F32)] * 2,
        ],
        compiler_params=pltpu.CompilerParams(
            dimension_semantics=("arbitrary",), vmem_limit_bytes=VMEM_LIMIT),
        name="ssm",
    )(U3, U3, are, aim, bre, bim, cre, cimn, d, wg_all, bg)


def _merge_kernel(ya_ref, yb_ref, yc_ref, ga_ref, gb_ref, gc_ref, x_ref, pa_ref, pb_ref, pc_ref, wo_ref,
                  o_ref):
    tm = x_ref.shape[0]
    mc = min(MERGE_ROWS, tm)
    for r in range(0, tm, mc):
        rows = slice(r, r + mc)
        m = _sigmoid(ga_ref[rows, :].astype(F32)) * _dot(ya_ref[rows, :], pa_ref[...])
        m += _sigmoid(gb_ref[rows, :].astype(F32)) * _dot(yb_ref[rows, :], pb_ref[...])
        m += _sigmoid(gc_ref[rows, :].astype(F32)) * _dot(yc_ref[rows, :], pc_ref[...])
        o_ref[rows, :] = x_ref[rows, :] + _dot(m.astype(BF16), wo_ref[...])


def _merge(ya, yb, yc2d, P, x2d, pa_all, pb_all, pc_all, wo_all, layer, B, S):
    tm = min(1024, S)
    nt = S // tm
    tok = lambda b, t: (b * nt + t, 0)
    of_layer = lambda b, t: (layer, 0, 0)
    return pl.pallas_call(
        _merge_kernel,
        out_shape=jax.ShapeDtypeStruct((B * S, D_MODEL), F32),
        grid=(B, nt),
        in_specs=[
            pl.BlockSpec((tm, RET_V_W), tok),
            pl.BlockSpec((tm, SB_W), tok),
            pl.BlockSpec((tm, SSM_W), lambda b, t: (t, b)),
            pl.BlockSpec((tm, D_MODEL), lambda b, t: (b * nt + t, P_GA)),
            pl.BlockSpec((tm, D_MODEL), lambda b, t: (b * nt + t, P_GB)),
            pl.BlockSpec((tm, D_MODEL), lambda b, t: (b * nt + t, P_GC)),
            pl.BlockSpec((tm, D_MODEL), tok),
            pl.BlockSpec((None, RET_V_W, D_MODEL), of_layer),
            pl.BlockSpec((None, SB_W, D_MODEL), of_layer),
            pl.BlockSpec((None, SSM_W, D_MODEL), of_layer),
            pl.BlockSpec((None, D_MODEL, D_MODEL), of_layer),
        ],
        out_specs=pl.BlockSpec((tm, D_MODEL), tok),
        compiler_params=pltpu.CompilerParams(
            dimension_semantics=("arbitrary", "arbitrary"), vmem_limit_bytes=VMEM_LIMIT),
        name="merge",
    )(ya, yb, yc2d, P, P, P, x2d, pa_all, pb_all, pc_all, wo_all)


def _rope_tables(S):
    half = RET_QK_DIM // 2
    inv_freq = ROPE_BASE ** (-jnp.arange(half, dtype=F32) / half)
    ang = jnp.arange(S, dtype=F32)[:, None] * inv_freq[None, :]
    cos, sin = jnp.cos(ang), jnp.sin(ang)
    return jnp.concatenate([cos, cos], axis=1), jnp.concatenate([-sin, sin], axis=1)


def _decay_tables():
    log_gamma = jnp.log1p(-jnp.exp2(-5.0 - jnp.arange(RET_HEADS, dtype=F32)))
    idx = jnp.arange(RET_CHUNK, dtype=F32)
    rel = idx[:, None] - idx[None, :]
    dec = jnp.where(rel >= 0, jnp.exp(jnp.maximum(rel, 0.0)[None] * log_gamma[:, None, None]), 0.0)
    kdec = jnp.exp((RET_CHUNK - 1 - idx)[None, :] * log_gamma[:, None])
    qdec = jnp.exp((idx + 1)[None, :] * log_gamma[:, None])
    full = lambda a: jnp.broadcast_to(a[:, :, None], (RET_HEADS, RET_CHUNK, RET_QK_DIM))
    chunk_decay = tuple(float(math.exp(RET_CHUNK * math.log1p(-2.0 ** (-5.0 - h)))) for h in range(RET_HEADS))
    return dec, full(qdec), full(kdec), chunk_decay


def _ssm_tables(a_re, a_im, log_dt, b_re, b_im, c_re, c_im):
    dt = jnp.exp(log_dt)[:, None]
    mag = jnp.exp(dt * a_re)
    ab_re = mag * jnp.cos(dt * a_im)
    ab_im = mag * jnp.sin(dt * a_im)
    den = a_re * a_re + a_im * a_im
    nr = ab_re - 1.0
    coef_re = (nr * a_re + ab_im * a_im) / den
    coef_im = (ab_im * a_re - nr * a_im) / den
    bb_re = coef_re[..., None] * b_re - coef_im[..., None] * b_im
    bb_im = coef_re[..., None] * b_im + coef_im[..., None] * b_re
    gl = SSM_GROUPS // 2
    eye = jnp.eye(gl, dtype=F32)

    def b_blocks(bb):
        return jnp.einsum('cgpm,gh->cgmhp', bb.reshape(2, gl, SSM_STATE, SSM_GROUP), eye).reshape(
            2, gl * SSM_GROUP, gl * SSM_STATE).astype(BF16)

    def c_blocks(cc):
        return jnp.einsum('cgmp,gh->cgphm', cc.reshape(2, gl, SSM_GROUP, SSM_STATE), eye).reshape(
            2, gl * SSM_STATE, gl * SSM_GROUP).astype(BF16)

    return (ab_re.reshape(1, SSM_STATES), ab_im.reshape(1, SSM_STATES),
            b_blocks(bb_re), b_blocks(bb_im), c_blocks(c_re), c_blocks(-c_im))


def _suffix_sum_matrix(tile):
    k = jnp.arange(tile)
    return -(k[:, None] > k[None, :]).astype(BF16)


def kernel(x, norm_g, w_in, ret_q_norm, ret_k_norm, ret_out_norm, sb_q_norm, sb_k_norm, ssm_a_re, ssm_a_im,
           ssm_log_dt, ssm_b_re, ssm_b_im, ssm_c_re, ssm_c_im, ssm_d, ssm_w_glu, ssm_b_glu, proj_a, proj_b,
           proj_c, w_out):
    B, S, D = x.shape
    depth = w_in.shape[0]
    assert D == D_MODEL and B == 16 and S % 128 == 0
    cosf, sinf = _rope_tables(S)
    dec, qd, kd, chunk_decay = _decay_tables()
    tri = _suffix_sum_matrix(min(SB_TILE, S))
    x2d = x.reshape(B * S, D)
    w_in_b, w_glu_b, proj_a_b, proj_b_b, proj_c_b, w_out_b = (
        w.astype(BF16) for w in (w_in, ssm_w_glu, proj_a, proj_b, proj_c, w_out))
    for l in range(depth):
        P, U = _inproj(x2d, norm_g[l].reshape(1, D), w_in_b, l, B, S)
        ya = _retention(P, cosf, sinf, dec, qd, kd, chunk_decay,
                        ret_q_norm[l], ret_k_norm[l] * (RET_QK_DIM ** -0.5),
                        ret_out_norm[l].reshape(1, -1), B, S)
        yb = _stick_breaking(P, jnp.tile(sb_q_norm[l], 2).reshape(1, -1),
                             jnp.tile(sb_k_norm[l], 2).reshape(1, -1), tri, B, S)
        are, aim, bre, bim, cre, cimn = _ssm_tables(ssm_a_re[l], ssm_a_im[l], ssm_log_dt[l], ssm_b_re[l],
                                                    ssm_b_im[l], ssm_c_re[l], ssm_c_im[l])
        yc = _ssm(U.reshape(S, B, COL_TILE), are, aim, bre, bim, cre, cimn,
                  ssm_d[l].reshape(1, -1), w_glu_b, l, ssm_b_glu[l].reshape(1, -1), B, S)
        x2d = _merge(ya, yb, yc.reshape(S, B * SSM_W), P, x2d, proj_a_b, proj_b_b, proj_c_b, w_out_b, l, B, S)
    return x2d.reshape(B, S, D)
```
